```python
import math
import jax, jax.numpy as jnp
from jax import lax
import numpy as np

D_MODEL = 2048
BATCH = 1
SEQ = 8192
DEPTH = 4
DEC_BATCH = 4
DEC_SEQ = 4096
PAST_LEN = 128

S5_WIDTH = D_MODEL // 2
S5_GROUP = 16
S5_GROUPS = S5_WIDTH // S5_GROUP
S5_STATE = 64
LRU_WIDTH = D_MODEL // 2
LRU_BLOCKS = 16
LRU_BLOCK = LRU_WIDTH // LRU_BLOCKS
LRU_C = 8.0
LRU_CONV = 4
LRU_CONV_LEFT = 2
FFN_HIDDEN = 3 * D_MODEL
FFN_CONV = 3
FFN_CONV_LEFT = 1
OFF_LRU_X = S5_WIDTH
OFF_LRU_G = S5_WIDTH + LRU_WIDTH
OFF_GATE_A = S5_WIDTH + 2 * LRU_WIDTH
OFF_GATE_B = OFF_GATE_A + D_MODEL
IN_COLS = OFF_GATE_B + D_MODEL
EPS = 1e-6
DT_MIN = 0.001
DT_MAX = 0.1

kernel_name = "hybrid_s5_rglru_bidir_encoder"


def _rmsnorm(x, g):
    xf = x.astype(jnp.float32)
    y = xf * lax.rsqrt(jnp.mean(xf * xf, axis=-1, keepdims=True) + EPS)
    return (y * g.astype(jnp.float32)).astype(x.dtype)


def _depthwise_conv(x, w, left):
    k = w.shape[0]
    s = x.shape[1]
    xp = jnp.pad(x, ((0, 0), (left, k - 1 - left), (0, 0)))
    y = xp[:, 0:s] * w[0]
    for j in range(1, k):
        y = y + xp[:, j:j + s] * w[j]
    return y


def _linear_scan_combine(e1, e2):
    a1, b1 = e1
    a2, b2 = e2
    return a1 * a2, a2 * b1 + b2


def _s5_direction(ug, a_re, a_im, log_dt, b_re, b_im, c_re, c_im, reverse):
    f32 = jnp.float32
    lam = lax.complex(a_re.astype(f32), a_im.astype(f32))
    dt = jnp.exp(log_dt.astype(f32))[:, None]
    a_bar = jnp.exp(lam * dt)
    b_bar = ((a_bar - 1.0) / lam)[:, :, None] * lax.complex(b_re.astype(f32), b_im.astype(f32))
    bu = jnp.einsum('bsgh,gph->bsgp', ug.astype(jnp.complex64), b_bar)
    a_all = jnp.broadcast_to(a_bar, bu.shape)
    _, h = lax.associative_scan(_linear_scan_combine, (a_all, bu), axis=1, reverse=reverse)
    c = lax.complex(c_re.astype(f32), c_im.astype(f32))
    return jnp.einsum('bsgp,ghp->bsgh', h, c).real


def _s5_branch(u, a_re, a_im, log_dt, b_re, b_im, c_re, c_im, d, w_glu):
    bsz, s, _ = u.shape
    uf = u.astype(jnp.float32)
    ug = uf.reshape(bsz, s, S5_GROUPS, S5_GROUP)
    y_f = _s5_direction(ug, a_re[0], a_im[0], log_dt[0], b_re[0], b_im[0], c_re[0], c_im[0], False)
    y_b = _s5_direction(ug, a_re[1], a_im[1], log_dt[1], b_re[1], b_im[1], c_re[1], c_im[1], True)
    y = (y_f + y_b).reshape(bsz, s, S5_WIDTH) + d.astype(jnp.float32) * uf
    y = jax.nn.gelu(y)
    y = y * jax.nn.sigmoid(y @ w_glu.astype(jnp.float32))
    return y.astype(u.dtype)


def _rglru_direction(xc, w_a, b_a, w_x, b_x, lam, reverse):
    f32 = jnp.float32
    bsz, s, _ = xc.shape
    xb = xc.reshape(bsz, s, LRU_BLOCKS, LRU_BLOCK)
    r = jax.nn.sigmoid(jnp.einsum('bshi,hij->bshj', xb, w_a.astype(f32)).reshape(bsz, s, LRU_WIDTH) + b_a.astype(f32))
    gi = jax.nn.sigmoid(jnp.einsum('bshi,hij->bshj', xb, w_x.astype(f32)).reshape(bsz, s, LRU_WIDTH) + b_x.astype(f32))
    log_a = -LRU_C * r * jax.nn.softplus(-lam.astype(f32))
    a = jnp.exp(log_a)
    b = jnp.sqrt(-jnp.expm1(2.0 * log_a)) * (gi * xc)
    _, h = lax.associative_scan(_linear_scan_combine, (a, b), axis=1, reverse=reverse)
    return h


def _rglru_branch(xr, gr, conv_w, conv_b, w_a, b_a, w_x, b_x, lam):
    xf = xr.astype(jnp.float32)
    xc = _depthwise_conv(xf, conv_w.astype(jnp.float32), LRU_CONV_LEFT) + conv_b.astype(jnp.float32)
    h = (_rglru_direction(xc, w_a[0], b_a[0], w_x[0], b_x[0], lam[0], False)
         + _rglru_direction(xc, w_a[1], b_a[1], w_x[1], b_x[1], lam[1], True))
    y = h * jax.nn.gelu(gr.astype(jnp.float32))
    return y.astype(xr.dtype)


def _mixer(xn, w_in, s5_a_re, s5_a_im, s5_log_dt, s5_b_re, s5_b_im, s5_c_re, s5_c_im, s5_d, s5_w_glu,
           lru_conv_w, lru_conv_b, lru_w_a, lru_b_a, lru_w_x, lru_b_x, lru_lambda,
           w_proj_a, w_proj_b, w_out):
    proj = xn @ w_in
    u = proj[..., :OFF_LRU_X]
    xr = proj[..., OFF_LRU_X:OFF_LRU_G]
    gr = proj[..., OFF_LRU_G:OFF_GATE_A]
    ga = proj[..., OFF_GATE_A:OFF_GATE_B]
    gb = proj[..., OFF_GATE_B:]
    ya = _s5_branch(u, s5_a_re, s5_a_im, s5_log_dt, s5_b_re, s5_b_im, s5_c_re, s5_c_im, s5_d, s5_w_glu)
    yb = _rglru_branch(xr, gr, lru_conv_w, lru_conv_b, lru_w_a, lru_b_a, lru_w_x, lru_b_x, lru_lambda)
    merged = jax.nn.sigmoid(ga) * (ya @ w_proj_a) + jax.nn.sigmoid(gb) * (yb @ w_proj_b)
    return merged @ w_out


def _conv_ffn(xn, w_up, conv_w, w_down):
    h = _depthwise_conv(xn @ w_up, conv_w, FFN_CONV_LEFT)
    gate = h[..., :FFN_HIDDEN]
    val = h[..., FFN_HIDDEN:]
    return (jax.nn.gelu(gate) * val) @ w_down


def _trunk(x, norm1_g, w_in, s5_a_re, s5_a_im, s5_log_dt, s5_b_re, s5_b_im, s5_c_re, s5_c_im, s5_d, s5_w_glu,
           lru_conv_w, lru_conv_b, lru_w_a, lru_b_a, lru_w_x, lru_b_x, lru_lambda,
           w_proj_a, w_proj_b, w_out, norm2_g, ffn_w_up, ffn_conv_w, ffn_w_down, final_g):
    for l in range(DEPTH):
        x = x + _mixer(_rmsnorm(x, norm1_g[l]), w_in[l], s5_a_re[l], s5_a_im[l], s5_log_dt[l],
                       s5_b_re[l], s5_b_im[l], s5_c_re[l], s5_c_im[l], s5_d[l], s5_w_glu[l],
                       lru_conv_w[l], lru_conv_b[l], lru_w_a[l], lru_b_a[l], lru_w_x[l], lru_b_x[l],
                       lru_lambda[l], w_proj_a[l], w_proj_b[l], w_out[l])
        x = x + _conv_ffn(_rmsnorm(x, norm2_g[l]), ffn_w_up[l], ffn_conv_w[l], ffn_w_down[l])
    return _rmsnorm(x, final_g)


def setup_inputs(seed: int = 0) -> dict:
    key = jax.random.key(seed)
    ks = jax.random.split(key, 32)
    f32 = jnp.float32
    L, D, G, P, H = DEPTH, D_MODEL, S5_GROUPS, S5_STATE, S5_GROUP
    W, NB, BLK, F = LRU_WIDTH, LRU_BLOCKS, LRU_BLOCK, FFN_HIDDEN

    def nrm(k, shape, scale):
        return jax.random.normal(k, shape, f32) * scale

    x_prompt = jax.random.normal(ks[0], (BATCH, SEQ, D), f32)
    x_sample = jax.random.normal(ks[1], (DEC_BATCH, DEC_SEQ, D), f32)
    norm1_g = 1.0 + nrm(ks[2], (L, D), 0.01)
    w_in = nrm(ks[3], (L, D, IN_COLS), D ** -0.5)
    n_idx = jnp.arange(P, dtype=f32)
    s5_a_re = -0.5 + nrm(ks[4], (L, 2, G, P), 0.01)
    s5_a_im = math.pi * n_idx + nrm(ks[5], (L, 2, G, P), 0.01)
    s5_log_dt = jax.random.uniform(ks[6], (L, 2, G), f32, math.log(DT_MIN), math.log(DT_MAX))
    s5_b_re = nrm(ks[7], (L, 2, G, P, H), (2.0 * H) ** -0.5)
    s5_b_im = nrm(ks[8], (L, 2, G, P, H), (2.0 * H) ** -0.5)
    s5_c_re = nrm(ks[9], (L, 2, G, H, P), (2.0 * P) ** -0.5)
    s5_c_im = nrm(ks[10], (L, 2, G, H, P), (2.0 * P) ** -0.5)
    s5_d = nrm(ks[11], (L, S5_WIDTH), 1.0)
    s5_w_glu = nrm(ks[12], (L, S5_WIDTH, S5_WIDTH), S5_WIDTH ** -0.5)
    lru_conv_w = nrm(ks[13], (L, LRU_CONV, W), LRU_CONV ** -0.5)
    lru_conv_b = nrm(ks[14], (L, W), 0.01)
    lru_w_a = nrm(ks[15], (L, 2, NB, BLK, BLK), BLK ** -0.5)
    lru_b_a = nrm(ks[16], (L, 2, W), 0.01)
    lru_w_x = nrm(ks[17], (L, 2, NB, BLK, BLK), BLK ** -0.5)
    lru_b_x = nrm(ks[18], (L, 2, W), 0.01)
    a_target = jax.random.uniform(ks[19], (L, 2, W), f32, 0.9, 0.999)
    s = a_target ** (1.0 / LRU_C)
    lru_lambda = jnp.log(s) - jnp.log1p(-s)
    w_proj_a = nrm(ks[20], (L, S5_WIDTH, D), S5_WIDTH ** -0.5)
    w_proj_b = nrm(ks[21], (L, W, D), W ** -0.5)
    w_out = nrm(ks[22], (L, D, D), D ** -0.5)
    norm2_g = 1.0 + nrm(ks[23], (L, D), 0.01)
    ffn_w_up = nrm(ks[24], (L, D, 2 * F), D ** -0.5)
    ffn_conv_w = nrm(ks[25], (L, FFN_CONV, 2 * F), FFN_CONV ** -0.5)
    ffn_w_down = nrm(ks[26], (L, F, D), F ** -0.5)
    final_g = 1.0 + nrm(ks[27], (D,), 0.01)
    return {"x_prompt": x_prompt, "x_sample": x_sample, "norm1_g": norm1_g, "w_in": w_in,
            "s5_a_re": s5_a_re, "s5_a_im": s5_a_im, "s5_log_dt": s5_log_dt,
            "s5_b_re": s5_b_re, "s5_b_im": s5_b_im, "s5_c_re": s5_c_re, "s5_c_im": s5_c_im,
            "s5_d": s5_d, "s5_w_glu": s5_w_glu, "lru_conv_w": lru_conv_w, "lru_conv_b": lru_conv_b,
            "lru_w_a": lru_w_a, "lru_b_a": lru_b_a, "lru_w_x": lru_w_x, "lru_b_x": lru_b_x,
            "lru_lambda": lru_lambda, "w_proj_a": w_proj_a, "w_proj_b": w_proj_b, "w_out": w_out,
            "norm2_g": norm2_g, "ffn_w_up": ffn_w_up, "ffn_conv_w": ffn_conv_w, "ffn_w_down": ffn_w_down,
            "final_g": final_g}


def reference(x_prompt, x_sample, norm1_g, w_in, s5_a_re, s5_a_im, s5_log_dt, s5_b_re, s5_b_im,
              s5_c_re, s5_c_im, s5_d, s5_w_glu, lru_conv_w, lru_conv_b, lru_w_a, lru_b_a, lru_w_x,
              lru_b_x, lru_lambda, w_proj_a, w_proj_b, w_out, norm2_g, ffn_w_up, ffn_conv_w,
              ffn_w_down, final_g):
    weights = (norm1_g, w_in, s5_a_re, s5_a_im, s5_log_dt, s5_b_re, s5_b_im, s5_c_re, s5_c_im, s5_d,
               s5_w_glu, lru_conv_w, lru_conv_b, lru_w_a, lru_b_a, lru_w_x, lru_b_x, lru_lambda,
               w_proj_a, w_proj_b, w_out, norm2_g, ffn_w_up, ffn_conv_w, ffn_w_down, final_g)
    y_prompt = _trunk(x_prompt, *weights)
    y_sample = _trunk(x_sample, *weights)
    return (y_prompt, y_sample)
```

```python
import functools
import math

import jax
import jax.numpy as jnp
from jax import lax
from jax.experimental import pallas as pl
from jax.experimental.pallas import tpu as pltpu

F32 = jnp.float32
BF16 = jnp.bfloat16

LANES = 128
NSEG = 16
TCH = 8
S5_H = 16
GROUPS_PER_BLOCK = LANES // S5_H
EPS = 1e-6
LRU_C = 8.0
VMEM_LIMIT = 56 * 1024 * 1024


def _cparams(sem, vmem=VMEM_LIMIT):
    return pltpu.CompilerParams(dimension_semantics=sem, vmem_limit_bytes=vmem)


def _resident(shape, imap):
    return pl.BlockSpec(shape, imap, pipeline_mode=pl.Buffered(1))


def _interleave(x):
    b, s, d = x.shape
    return x.reshape(b, NSEG, s // NSEG, d).transpose(0, 2, 1, 3).reshape(b, s, d)


def _deinterleave(x):
    b, s, d = x.shape
    return x.reshape(b, s // NSEG, NSEG, d).transpose(0, 2, 1, 3).reshape(b, s, d)


def _rms(xf, g):
    return xf * lax.rsqrt(jnp.mean(xf * xf, axis=-1, keepdims=True) + EPS) * g


def _inproj_body(x_ref, g_ref, w_ref, o_ref, xn_ref):
    @pl.when(pl.program_id(2) == 0)
    def _():
        xn_ref[...] = _rms(x_ref[...], g_ref[...]).astype(BF16)

    o_ref[...] = jnp.dot(xn_ref[...], w_ref[...], preferred_element_type=F32).astype(o_ref.dtype)


def _inproj(x, g, w, layer, tm, tn):
    b, s, d = x.shape
    n = w.shape[-1]
    tm = min(tm, s)
    return pl.pallas_call(
        _inproj_body,
        grid=(b, s // tm, n // tn),
        in_specs=[
            pl.BlockSpec((None, tm, d), lambda bi, i, j: (bi, i, 0)),
            pl.BlockSpec((None, 1, d), lambda bi, i, j: (layer, 0, 0)),
            pl.BlockSpec((None, d, tn), lambda bi, i, j: (layer, 0, j)),
        ],
        out_specs=pl.BlockSpec((None, tm, tn), lambda bi, i, j: (bi, i, j)),
        out_shape=jax.ShapeDtypeStruct((b, s, n), BF16),
        scratch_shapes=[pltpu.VMEM((tm, d), BF16)],
        compiler_params=_cparams(("parallel", "parallel", "arbitrary")),
        name="inproj",
    )(x, g, w)


def _s5_body(u_ref, m_ref, win_ref, wout_ref, av_ref, o_ref,
             ucat_ref, s_ref, h_ref, e_ref, g_ref, *, nq):
    nc = nq * NSEG
    ns = 4 * LANES
    for t in range(TCH):
        ucat_ref[:, t * LANES:(t + 1) * LANES] = u_ref[:, t].reshape(nc, LANES)
    ucat = ucat_ref[...]
    s_ref[...] = jnp.dot(ucat, win_ref[...], preferred_element_type=F32)

    a_fr, a_fi = av_ref[0:1, :], av_ref[1:2, :]
    a_br, a_bi = av_ref[2:3, :], av_ref[3:4, :]

    def scan(init, store):
        def step(q, carry):
            hfr, hfi, hbr, hbi = carry
            rf = pl.multiple_of(q * NSEG, NSEG)
            rb = pl.multiple_of((nq - 1 - q) * NSEG, NSEG)
            if store:
                h_ref[pl.ds(rf, NSEG), 0 * ns:1 * ns] = hfr.astype(BF16)
                h_ref[pl.ds(rf, NSEG), 1 * ns:2 * ns] = hfi.astype(BF16)
                h_ref[pl.ds(rb, NSEG), 2 * ns:3 * ns] = hbr.astype(BF16)
                h_ref[pl.ds(rb, NSEG), 3 * ns:4 * ns] = hbi.astype(BF16)
            sfr = s_ref[pl.ds(rf, NSEG), 0 * ns:1 * ns]
            sfi = s_ref[pl.ds(rf, NSEG), 1 * ns:2 * ns]
            sbr = s_ref[pl.ds(rb, NSEG), 2 * ns:3 * ns]
            sbi = s_ref[pl.ds(rb, NSEG), 3 * ns:4 * ns]
            return (a_fr * hfr - a_fi * hfi + sfr, a_fr * hfi + a_fi * hfr + sfi,
                    a_br * hbr - a_bi * hbi + sbr, a_br * hbi + a_bi * hbr + sbi)
        return lax.fori_loop(0, nq, step, init)

    zero = jnp.zeros((NSEG, ns), F32)
    efr, efi, ebr, ebi = scan((zero, zero, zero, zero), False)
    e_ref[:, 0 * ns:1 * ns] = efr
    e_ref[:, 1 * ns:2 * ns] = efi
    e_ref[:, 2 * ns:3 * ns] = ebr
    e_ref[:, 3 * ns:4 * ns] = ebi

    p_fr, p_fi = av_ref[4:5, :], av_ref[5:6, :]
    p_br, p_bi = av_ref[6:7, :], av_ref[7:8, :]
    zrow = jnp.zeros((1, ns), F32)
    g_ref[0:1, 0:2 * ns] = jnp.zeros((1, 2 * ns), F32)
    g_ref[NSEG - 1:NSEG, 2 * ns:4 * ns] = jnp.zeros((1, 2 * ns), F32)
    gr, gi = zrow, zrow
    for j in range(NSEG - 1):
        er, ei = e_ref[j:j + 1, 0 * ns:1 * ns], e_ref[j:j + 1, 1 * ns:2 * ns]
        gr, gi = p_fr * gr - p_fi * gi + er, p_fr * gi + p_fi * gr + ei
        g_ref[j + 1:j + 2, 0 * ns:1 * ns] = gr
        g_ref[j + 1:j + 2, 1 * ns:2 * ns] = gi
    gr, gi = zrow, zrow
    for j in range(NSEG - 1, 0, -1):
        er, ei = e_ref[j:j + 1, 2 * ns:3 * ns], e_ref[j:j + 1, 3 * ns:4 * ns]
        gr, gi = p_br * gr - p_bi * gi + er, p_br * gi + p_bi * gr + ei
        g_ref[j - 1:j, 2 * ns:3 * ns] = gr
        g_ref[j - 1:j, 3 * ns:4 * ns] = gi

    scan((g_ref[:, 0 * ns:1 * ns], g_ref[:, 1 * ns:2 * ns],
          g_ref[:, 2 * ns:3 * ns], g_ref[:, 3 * ns:4 * ns]), True)

    y = jnp.dot(ucat, m_ref[...], preferred_element_type=F32)
    y = y + jnp.dot(h_ref[...], wout_ref[...], preferred_element_type=F32)
    for t in range(TCH):
        o_ref[:, t] = y[:, t * LANES:(t + 1) * LANES].reshape(nq, NSEG, LANES)


def _s5_mix(proj, col0, m, win, wout, av, layer, width):
    b, s, n = proj.shape
    ncb = width // LANES
    nq = s // (NSEG * TCH)
    nc = nq * NSEG
    kt = TCH * LANES
    nst = 16 * LANES
    proj5 = proj.reshape(b, nq, TCH, NSEG, n)
    out = pl.pallas_call(
        functools.partial(_s5_body, nq=nq),
        grid=(ncb, b),
        in_specs=[
            pl.BlockSpec((None, nq, TCH, NSEG, LANES), lambda c, bi: (bi, 0, 0, 0, col0 + c)),
            _resident((None, None, kt, kt), lambda c, bi: (layer, c, 0, 0)),
            _resident((None, None, kt, nst), lambda c, bi: (layer, c, 0, 0)),
            _resident((None, None, nst, kt), lambda c, bi: (layer, c, 0, 0)),
            pl.BlockSpec((None, None, 8, 4 * LANES), lambda c, bi: (layer, c, 0, 0)),
        ],
        out_specs=pl.BlockSpec((None, nq, TCH, NSEG, LANES), lambda c, bi: (bi, 0, 0, 0, c)),
        out_shape=jax.ShapeDtypeStruct((b, nq, TCH, NSEG, width), F32),
        scratch_shapes=[
            pltpu.VMEM((nc, kt), BF16),
            pltpu.VMEM((nc, nst), F32),
            pltpu.VMEM((nc, nst), BF16),
            pltpu.VMEM((NSEG, nst), F32),
            pltpu.VMEM((NSEG, nst), F32),
        ],
        compiler_params=_cparams(("parallel", "parallel")),
        name="s5_mix",
    )(proj5, m, win, wout, av)
    return out.reshape(b, s, width)


def _rglru_body(xr_ref, gr_ref, cw_ref, par_ref, wg_ref, o_ref,
                pad_ref, tmp_ref, af_ref, bf_ref, ab_ref, bb_ref, *, s, tt):
    r = NSEG
    ls = s // r
    pad_ref[2 * r:2 * r + s, :] = xr_ref[...].astype(F32)
    tmp_ref[...] = jnp.zeros((3 * r, LANES), F32)
    tmp_ref[r:2 * r, :] = xr_ref[s - 2 * r:s - r, :].astype(F32)
    pad_ref[0:r, :] = tmp_ref[r - 1:2 * r - 1, :]
    tmp_ref[r:2 * r, :] = xr_ref[s - r:s, :].astype(F32)
    pad_ref[r:2 * r, :] = tmp_ref[r - 1:2 * r - 1, :]
    tmp_ref[r:2 * r, :] = xr_ref[0:r, :].astype(F32)
    pad_ref[2 * r + s:3 * r + s, :] = tmp_ref[r + 1:2 * r + 1, :]

    cb = par_ref[0:1, :]
    b_af, b_xf, b_ab, b_xb = (par_ref[k:k + 1, :] for k in range(1, 5))
    c_f = -LRU_C * jax.nn.softplus(-par_ref[5:6, :])
    c_b = -LRU_C * jax.nn.softplus(-par_ref[6:7, :])

    def gates(k, _):
        r0 = pl.multiple_of(k * tt, tt)
        xc = cb + cw_ref[0:1, :] * pad_ref[pl.ds(r0, tt), :]
        for tap in range(1, 4):
            xc = xc + cw_ref[tap:tap + 1, :] * pad_ref[pl.ds(r0 + tap * r, tt), :]
        z = jnp.dot(xc.astype(BF16), wg_ref[...], preferred_element_type=F32)
        for (a_ref, b_ref, c, ba, bx, off) in ((af_ref, bf_ref, c_f, b_af, b_xf, 0),
                                               (ab_ref, bb_ref, c_b, b_ab, b_xb, 2)):
            rg = jax.nn.sigmoid(z[:, off * LANES:(off + 1) * LANES] + ba)
            gi = jax.nn.sigmoid(z[:, (off + 1) * LANES:(off + 2) * LANES] + bx)
            a = jnp.exp(c * rg)
            a_ref[pl.ds(r0, tt), :] = a
            b_ref[pl.ds(r0, tt), :] = jnp.sqrt(1.0 - a * a) * (gi * xc)
        return 0

    lax.fori_loop(0, s // tt, gates, 0)

    def rows(i):
        return pl.ds(pl.multiple_of(i * r, r), r)

    def local(i, carry):
        hf, pf, hb, pb = carry
        ib = ls - 1 - i
        a_f, a_b = af_ref[rows(i), :], ab_ref[rows(ib), :]
        return (a_f * hf + bf_ref[rows(i), :], a_f * pf, a_b * hb + bb_ref[rows(ib), :], a_b * pb)

    zero = jnp.zeros((r, LANES), F32)
    one = jnp.ones((r, LANES), F32)
    ef, pf, eb, pb = lax.fori_loop(0, ls, local, (zero, one, zero, one))

    tmp_ref[0:r, :] = ef
    tmp_ref[r:2 * r, :] = pf
    g = jnp.zeros((1, LANES), F32)
    pad_ref[0:1, :] = g
    for j in range(r - 1):
        g = tmp_ref[r + j:r + j + 1, :] * g + tmp_ref[j:j + 1, :]
        pad_ref[j + 1:j + 2, :] = g
    gf = pad_ref[0:r, :]
    tmp_ref[0:r, :] = eb
    tmp_ref[r:2 * r, :] = pb
    g = jnp.zeros((1, LANES), F32)
    pad_ref[r - 1:r, :] = g
    for j in range(r - 1, 0, -1):
        g = tmp_ref[r + j:r + j + 1, :] * g + tmp_ref[j:j + 1, :]
        pad_ref[j - 1:j, :] = g
    gb = pad_ref[0:r, :]

    def fwd(i, h):
        h = af_ref[rows(i), :] * h + bf_ref[rows(i), :]
        bf_ref[rows(i), :] = h
        return h

    lax.fori_loop(0, ls, fwd, gf)

    def bwd(i, h):
        ib = ls - 1 - i
        h = ab_ref[rows(ib), :] * h + bb_ref[rows(ib), :]
        gate = jax.nn.gelu(gr_ref[rows(ib), :].astype(F32))
        o_ref[rows(ib), :] = ((bf_ref[rows(ib), :] + h) * gate).astype(o_ref.dtype)
        return h

    lax.fori_loop(0, ls, bwd, gb)


def _rglru(proj, col_x, col_g, conv_w, par, wg, layer, width):
    b, s, n = proj.shape
    ncb = width // LANES
    tt = min(512, s)
    return pl.pallas_call(
        functools.partial(_rglru_body, s=s, tt=tt),
        grid=(ncb, b),
        in_specs=[
            pl.BlockSpec((None, s, LANES), lambda c, bi: (bi, 0, col_x + c)),
            pl.BlockSpec((None, s, LANES), lambda c, bi: (bi, 0, col_g + c)),
            pl.BlockSpec((None, 4, LANES), lambda c, bi: (layer, 0, c)),
            pl.BlockSpec((None, None, 8, LANES), lambda c, bi: (layer, c, 0, 0)),
            pl.BlockSpec((None, None, LANES, 4 * LANES), lambda c, bi: (layer, c, 0, 0)),
        ],
        out_specs=pl.BlockSpec((None, s, LANES), lambda c, bi: (bi, 0, c)),
        out_shape=jax.ShapeDtypeStruct((b, s, width), BF16),
        scratch_shapes=[
            pltpu.VMEM((s + 3 * NSEG, LANES), F32),
            pltpu.VMEM((3 * NSEG, LANES), F32),
            pltpu.VMEM((s, LANES), F32),
            pltpu.VMEM((s, LANES), F32),
            pltpu.VMEM((s, LANES), F32),
            pltpu.VMEM((s, LANES), F32),
        ],
        compiler_params=_cparams(("parallel", "parallel")),
        name="rglru",
    )(proj, proj, conv_w, par, wg)


def _postmix_body(x_ref, ys_ref, u_ref, yb_ref, ga_ref, gb_ref, d_ref,
                  wglu_ref, wa_ref, wb_ref, wo_ref, o_ref):
    y = jax.nn.gelu(ys_ref[...] + d_ref[...] * u_ref[...].astype(F32))
    z = jnp.dot(y.astype(BF16), wglu_ref[...], preferred_element_type=F32)
    ya = (y * jax.nn.sigmoid(z)).astype(BF16)
    pa = jnp.dot(ya, wa_ref[...], preferred_element_type=F32)
    pb = jnp.dot(yb_ref[...], wb_ref[...], preferred_element_type=F32)
    merged = (jax.nn.sigmoid(ga_ref[...].astype(F32)) * pa
              + jax.nn.sigmoid(gb_ref[...].astype(F32)) * pb)
    o_ref[...] = x_ref[...] + jnp.dot(merged.astype(BF16), wo_ref[...], preferred_element_type=F32)


def _postmix(x, ys, proj, yb, d, wglu, wa, wb, wo, layer, tm):
    b, s, dm = x.shape
    w5 = ys.shape[-1]
    wl = yb.shape[-1]
    tm = min(tm, s)
    u_blk = (2 * dm) // w5
    row = lambda bi, i: (bi, i, 0)
    return pl.pallas_call(
        _postmix_body,
        grid=(b, s // tm),
        in_specs=[
            pl.BlockSpec((None, tm, dm), row),
            pl.BlockSpec((None, tm, w5), row),
            pl.BlockSpec((None, tm, w5), lambda bi, i: (bi, i, u_blk)),
            pl.BlockSpec((None, tm, wl), row),
            pl.BlockSpec((None, tm, dm), lambda bi, i: (bi, i, 0)),
            pl.BlockSpec((None, tm, dm), lambda bi, i: (bi, i, 1)),
            pl.BlockSpec((None, 1, w5), lambda bi, i: (layer, 0, 0)),
            _resident((None, w5, w5), lambda bi, i: (layer, 0, 0)),
            _resident((None, w5, dm), lambda bi, i: (layer, 0, 0)),
            _resident((None, wl, dm), lambda bi, i: (layer, 0, 0)),
            _resident((None, dm, dm), lambda bi, i: (layer, 0, 0)),
        ],
        out_specs=pl.BlockSpec((None, tm, dm), row),
        out_shape=jax.ShapeDtypeStruct((b, s, dm), F32),
        compiler_params=_cparams(("parallel", "parallel")),
        name="postmix",
    )(x, ys, proj, yb, proj, proj, d, wglu, wa, wb, wo)


def _ffn_body(x_ref, xp_ref, xq_ref, g_ref, wug_ref, wuv_ref, cwg_ref, cwv_ref, wd_ref, o_ref,
              xn_ref, edge_ref, *, tm):
    r = NSEG
    i = pl.program_id(1)

    @pl.when(pl.program_id(2) == 0)
    def _():
        g = g_ref[...]
        xn_ref[r:r + tm, :] = _rms(x_ref[...], g).astype(BF16)
        edge_ref[...] = jnp.zeros(edge_ref.shape, F32)
        prev = _rms(xp_ref[...], g)
        edge_ref[r:2 * r, :] = prev
        prev = jnp.where(i == 0, edge_ref[r - 1:2 * r - 1, :], prev)
        xn_ref[0:r, :] = prev.astype(BF16)
        nxt = _rms(xq_ref[...], g)
        edge_ref[r:2 * r, :] = nxt
        nxt = jnp.where(i == pl.num_programs(1) - 1, edge_ref[r + 1:2 * r + 1, :], nxt)
        xn_ref[r + tm:2 * r + tm, :] = nxt.astype(BF16)
        o_ref[...] = x_ref[...]

    xn = xn_ref[...]

    def conv(w_ref, cw_ref):
        h = jnp.dot(xn, w_ref[...], preferred_element_type=F32)
        return (cw_ref[0:1, :] * h[0:tm] + cw_ref[1:2, :] * h[r:r + tm]
                + cw_ref[2:3, :] * h[2 * r:2 * r + tm])

    act = (jax.nn.gelu(conv(wug_ref, cwg_ref)) * conv(wuv_ref, cwv_ref)).astype(BF16)
    o_ref[...] += jnp.dot(act, wd_ref[...], preferred_element_type=F32)


def _ffn(x, g, wup, cw, wdown, layer, tm, tf):
    b, s, dm = x.shape
    fh = wdown.shape[1]
    tm = min(tm, s)
    tf = min(tf, fh)
    nf = fh // tf
    gpt = tm // NSEG
    ng = s // NSEG
    return pl.pallas_call(
        functools.partial(_ffn_body, tm=tm),
        grid=(b, s // tm, nf),
        in_specs=[
            pl.BlockSpec((None, tm, dm), lambda bi, i, f: (bi, i, 0)),
            pl.BlockSpec((None, NSEG, dm), lambda bi, i, f: (bi, lax.rem(i * gpt + ng - 1, ng), 0)),
            pl.BlockSpec((None, NSEG, dm), lambda bi, i, f: (bi, lax.rem((i + 1) * gpt, ng), 0)),
            pl.BlockSpec((None, 1, dm), lambda bi, i, f: (layer, 0, 0)),
            pl.BlockSpec((None, dm, tf), lambda bi, i, f: (layer, 0, f)),
            pl.BlockSpec((None, dm, tf), lambda bi, i, f: (layer, 0, nf + f)),
            pl.BlockSpec((None, 3, tf), lambda bi, i, f: (layer, 0, f)),
            pl.BlockSpec((None, 3, tf), lambda bi, i, f: (layer, 0, nf + f)),
            pl.BlockSpec((None, tf, dm), lambda bi, i, f: (layer, f, 0)),
        ],
        out_specs=pl.BlockSpec((None, tm, dm), lambda bi, i, f: (bi, i, 0)),
        out_shape=jax.ShapeDtypeStruct((b, s, dm), F32),
        scratch_shapes=[pltpu.VMEM((tm + 2 * NSEG, dm), BF16), pltpu.VMEM((3 * NSEG, dm), F32)],
        compiler_params=_cparams(("parallel", "parallel", "arbitrary")),
        name="ffn",
    )(x, x, x, g, wup, wup, cw, cw, wdown)


def _final_body(x_ref, g_ref, o_ref):
    o_ref[...] = _rms(x_ref[...], g_ref[...])


def _final_norm(x, g, tm):
    b, s, dm = x.shape
    tm = min(tm, s)
    return pl.pallas_call(
        _final_body,
        grid=(b, s // tm),
        in_specs=[pl.BlockSpec((None, tm, dm), lambda bi, i: (bi, i, 0)),
                  pl.BlockSpec((1, dm), lambda bi, i: (0, 0))],
        out_specs=pl.BlockSpec((None, tm, dm), lambda bi, i: (bi, i, 0)),
        out_shape=jax.ShapeDtypeStruct((b, s, dm), F32),
        compiler_params=_cparams(("parallel", "parallel")),
        name="final_norm",
    )(x, g)


def _block_diag_embed(w, nblk, spec_in, spec_out):
    return jnp.einsum(spec_in + ",gh->" + spec_out, w, jnp.eye(nblk, dtype=w.dtype))


def _prep_s5(a_re, a_im, log_dt, b_re, b_im, c_re, c_im, seg_lens):
    nl, _, ng, npz = a_re.shape
    hh = b_re.shape[-1]
    gb = GROUPS_PER_BLOCK
    ncb = ng // gb
    t = TCH
    hi = lax.Precision.HIGHEST
    lam = lax.complex(a_re.astype(F32), a_im.astype(F32))
    z = lam * jnp.exp(log_dt.astype(F32))[..., None]
    def powers(kvec):
        return jnp.exp(z[:, :, :, None, :] * jnp.asarray(list(kvec), F32)[:, None])

    apow = powers(range(t + 1))
    bbar = ((jnp.exp(z) - 1.0) / lam)[..., None] * lax.complex(b_re.astype(F32), b_im.astype(F32))
    c = lax.complex(c_re.astype(F32), c_im.astype(F32))

    ca = c[:, :, :, None, :, :] * apow[:, :, :, :t, None, :]
    taps = (jnp.einsum("ldgkip,ldgpj->ldgkij", jnp.real(ca), jnp.real(bbar), precision=hi)
            - jnp.einsum("ldgkip,ldgpj->ldgkij", jnp.imag(ca), jnp.imag(bbar), precision=hi))
    kf, kb = taps[:, 0], taps[:, 1]
    lag = jnp.arange(t)[None, :] - jnp.arange(t)[:, None]
    sel = lag[None, None, :, :, None, None]
    kf_l, kb_l = kf[:, :, jnp.abs(lag)], kb[:, :, jnp.abs(lag)]
    mg = jnp.where(sel > 0, kf_l, jnp.where(sel < 0, kb_l, kf_l + kb_l))
    mg = mg.transpose(0, 1, 2, 5, 3, 4).reshape(nl, ncb, gb, t, hh, t, hh)
    m = _block_diag_embed(mg, gb, "lcgsjti", "lcsgjthi").reshape(nl, ncb, t * LANES, t * LANES)

    adown = powers(range(t - 1, -1, -1))
    wf = adown[:, 0, :, :, None, :] * bbar[:, 0].transpose(0, 1, 3, 2)[:, :, None]
    wb = apow[:, 1, :, :t, None, :] * bbar[:, 1].transpose(0, 1, 3, 2)[:, :, None]
    win = jnp.stack([jnp.real(wf), jnp.imag(wf), jnp.real(wb), jnp.imag(wb)], axis=4)
    win = win.reshape(nl, ncb, gb, t, hh, 4, npz)
    win = _block_diag_embed(win, gb, "lcgsjkp", "lcsgjkhp").reshape(nl, ncb, t * LANES, 4 * gb * npz)

    cf = c[:, 0].transpose(0, 1, 3, 2)[:, :, :, None, :]
    cbk = c[:, 1].transpose(0, 1, 3, 2)[:, :, :, None, :]
    of = cf * apow[:, 0, :, 1:t + 1, :].transpose(0, 1, 3, 2)[..., None]
    ob = cbk * powers(range(t, 0, -1))[:, 1].transpose(0, 1, 3, 2)[..., None]
    wout = jnp.stack([jnp.real(of), -jnp.imag(of), jnp.real(ob), -jnp.imag(ob)], axis=2)
    wout = wout.reshape(nl, ncb, gb, 4, npz, t, hh)
    wout = _block_diag_embed(wout, gb, "lcgkpti", "lckgpthi").reshape(nl, ncb, 4 * gb * npz, t * LANES)

    a_t = apow[:, :, :, t, :]

    def vec(v):
        v = jnp.stack([jnp.real(v[:, 0]), jnp.imag(v[:, 0]), jnp.real(v[:, 1]), jnp.imag(v[:, 1])], axis=1)
        return v.reshape(nl, 4, ncb, gb * npz).transpose(0, 2, 1, 3)

    avs = [jnp.concatenate([vec(a_t), vec(jnp.exp(z * float(ls)))], axis=2) for ls in seg_lens]
    return m.astype(BF16), win.astype(BF16), wout.astype(BF16), avs


def _prep_lru(conv_b, w_a, b_a, w_x, b_x, lam):
    nl, _, nb, blk, _ = w_a.shape
    width = nb * blk
    ncb = width // LANES
    per = LANES // blk

    def bd(w):
        w = w.reshape(nl, ncb, per, blk, blk)
        return _block_diag_embed(w, per, "lcgij", "lcgihj").reshape(nl, ncb, LANES, LANES)

    wg = jnp.concatenate([bd(w_a[:, 0]), bd(w_x[:, 0]), bd(w_a[:, 1]), bd(w_x[:, 1])], axis=-1)
    rows = [conv_b, b_a[:, 0], b_x[:, 0], b_a[:, 1], b_x[:, 1], lam[:, 0], lam[:, 1],
            jnp.zeros_like(conv_b)]
    par = jnp.stack(rows, axis=1).astype(F32)
    par = par.reshape(nl, 8, ncb, LANES).transpose(0, 2, 1, 3)
    return wg.astype(BF16), par


def kernel(x_prompt, x_sample, norm1_g, w_in, s5_a_re, s5_a_im, s5_log_dt, s5_b_re, s5_b_im, s5_c_re, s5_c_im, s5_d, s5_w_glu, lru_conv_w, lru_conv_b, lru_w_a, lru_b_a, lru_w_x, lru_b_x, lru_lambda, w_proj_a, w_proj_b, w_out, norm2_g, ffn_w_up, ffn_conv_w, ffn_w_down, final_g):
    depth, dm, _ = w_in.shape
    w5 = s5_d.shape[-1]
    wl = lru_conv_b.shape[-1]
    groups = (x_prompt, x_sample)
    seg_lens = [x.shape[1] // NSEG for x in groups]

    w_in_p = jnp.concatenate([w_in[..., w5 + 2 * wl:], w_in[..., :w5 + 2 * wl]], axis=-1).astype(BF16)
    col_u = (2 * dm) // LANES
    col_x = col_u + w5 // LANES
    col_g = col_x + wl // LANES
    m, win, wout, avs = _prep_s5(s5_a_re, s5_a_im, s5_log_dt, s5_b_re, s5_b_im, s5_c_re, s5_c_im, seg_lens)
    wg, par = _prep_lru(lru_conv_b, lru_w_a, lru_b_a, lru_w_x, lru_b_x, lru_lambda)
    conv_w = lru_conv_w.astype(F32)
    wglu, wa, wb, wo = (w.astype(BF16) for w in (s5_w_glu, w_proj_a, w_proj_b, w_out))
    wup, wdown = ffn_w_up.astype(BF16), ffn_w_down.astype(BF16)
    ffn_cw = ffn_conv_w.astype(F32)
    g1 = norm1_g.astype(F32)[:, None, :]
    g2 = norm2_g.astype(F32)[:, None, :]
    d5 = s5_d.astype(F32)[:, None, :]
    gfin = final_g.astype(F32)[None, :]

    outs = []
    for x, av in zip(groups, avs):
        x = _interleave(x.astype(F32))
        for l in range(depth):
            proj = _inproj(x, g1, w_in_p, l, tm=512, tn=min(1024, w5))
            ys = _s5_mix(proj, col_u, m, win, wout, av, l, w5)
            yb = _rglru(proj, col_x, col_g, conv_w, par, wg, l, wl)
            x = _postmix(x, ys, proj, yb, d5, wglu, wa, wb, wo, l, tm=256)
            x = _ffn(x, g2, wup, ffn_cw, wdown, l, tm=512, tf=512)
        outs.append(_deinterleave(_final_norm(x, gfin, tm=512)))
    return tuple(outs)
```

```python
import functools
import math

import jax
import jax.numpy as jnp
from jax import lax
from jax.experimental import pallas as pl
from jax.experimental.pallas import tpu as pltpu

F32 = jnp.float32
BF16 = jnp.bfloat16

LANES = 128
NSEG = 16
TCH = 8
S5_H = 16
GROUPS_PER_BLOCK = LANES // S5_H
EPS = 1e-6
LRU_C = 8.0
LOG2_E = 1.0 / math.log(2.0)
VMEM_LIMIT = 56 * 1024 * 1024


def _cparams(sem, vmem=VMEM_LIMIT):
    return pltpu.CompilerParams(dimension_semantics=sem, vmem_limit_bytes=vmem)


def _resident(shape, imap):
    return pl.BlockSpec(shape, imap, pipeline_mode=pl.Buffered(1))


def _interleave(x):
    b, s, d = x.shape
    return x.reshape(b, NSEG, s // NSEG, d).transpose(0, 2, 1, 3).reshape(b, s, d)


def _deinterleave(x):
    b, s, d = x.shape
    return x.reshape(b, s // NSEG, NSEG, d).transpose(0, 2, 1, 3).reshape(b, s, d)


SCAN_UNROLL = 8


def _sigmoid(x):
    return 0.5 * jnp.tanh(0.5 * x) + 0.5


def _rms(xf, g):
    return xf * lax.rsqrt(jnp.mean(xf * xf, axis=-1, keepdims=True) + EPS) * g


def _inproj_body(x_ref, g_ref, w_ref, o_ref, xn_ref):
    @pl.when(pl.program_id(2) == 0)
    def _():
        xn_ref[...] = _rms(x_ref[...], g_ref[...]).astype(BF16)

    o_ref[...] = jnp.dot(xn_ref[...], w_ref[...], preferred_element_type=F32).astype(o_ref.dtype)


def _inproj(x, g, w, layer, tm, tn):
    b, s, d = x.shape
    n = w.shape[-1]
    tm = min(tm, s)
    return pl.pallas_call(
        _inproj_body,
        grid=(b, s // tm, n // tn),
        in_specs=[
            pl.BlockSpec((None, tm, d), lambda bi, i, j: (bi, i, 0)),
            pl.BlockSpec((None, 1, d), lambda bi, i, j: (layer, 0, 0)),
            pl.BlockSpec((None, d, tn), lambda bi, i, j: (layer, 0, j)),
        ],
        out_specs=pl.BlockSpec((None, tm, tn), lambda bi, i, j: (bi, i, j)),
        out_shape=jax.ShapeDtypeStruct((b, s, n), BF16),
        scratch_shapes=[pltpu.VMEM((tm, d), BF16)],
        compiler_params=_cparams(("parallel", "parallel", "arbitrary")),
        name="inproj",
    )(x, g, w)


def _s5_body(u_ref, m_ref, win_ref, wout_ref, av_ref, o_ref,
             ucat_ref, s_ref, h_ref, e_ref, g_ref, *, nq):
    nc = nq * NSEG
    ns = 4 * LANES
    for t in range(TCH):
        ucat_ref[:, t * LANES:(t + 1) * LANES] = u_ref[:, t].reshape(nc, LANES)
    ucat = ucat_ref[...]
    s_ref[...] = jnp.dot(ucat, win_ref[...], preferred_element_type=F32)

    a_fr, a_fi = av_ref[0:1, :], av_ref[1:2, :]
    a_br, a_bi = av_ref[2:3, :], av_ref[3:4, :]

    def scan(init, store):
        def step(q, carry):
            hfr, hfi, hbr, hbi = carry
            rf = pl.multiple_of(q * NSEG, NSEG)
            rb = pl.multiple_of((nq - 1 - q) * NSEG, NSEG)
            if store:
                h_ref[pl.ds(rf, NSEG), 0 * ns:1 * ns] = hfr.astype(BF16)
                h_ref[pl.ds(rf, NSEG), 1 * ns:2 * ns] = hfi.astype(BF16)
                h_ref[pl.ds(rb, NSEG), 2 * ns:3 * ns] = hbr.astype(BF16)
                h_ref[pl.ds(rb, NSEG), 3 * ns:4 * ns] = hbi.astype(BF16)
            sfr = s_ref[pl.ds(rf, NSEG), 0 * ns:1 * ns]
            sfi = s_ref[pl.ds(rf, NSEG), 1 * ns:2 * ns]
            sbr = s_ref[pl.ds(rb, NSEG), 2 * ns:3 * ns]
            sbi = s_ref[pl.ds(rb, NSEG), 3 * ns:4 * ns]
            return (a_fr * hfr - a_fi * hfi + sfr, a_fr * hfi + a_fi * hfr + sfi,
                    a_br * hbr - a_bi * hbi + sbr, a_br * hbi + a_bi * hbr + sbi)
        return lax.fori_loop(0, nq, step, init)

    zero = jnp.zeros((NSEG, ns), F32)
    efr, efi, ebr, ebi = scan((zero, zero, zero, zero), False)
    e_ref[:, 0 * ns:1 * ns] = efr
    e_ref[:, 1 * ns:2 * ns] = efi
    e_ref[:, 2 * ns:3 * ns] = ebr
    e_ref[:, 3 * ns:4 * ns] = ebi

    p_fr, p_fi = av_ref[4:5, :], av_ref[5:6, :]
    p_br, p_bi = av_ref[6:7, :], av_ref[7:8, :]
    zrow = jnp.zeros((1, ns), F32)
    g_ref[0:1, 0:2 * ns] = jnp.zeros((1, 2 * ns), F32)
    g_ref[NSEG - 1:NSEG, 2 * ns:4 * ns] = jnp.zeros((1, 2 * ns), F32)
    gr, gi = zrow, zrow
    for j in range(NSEG - 1):
        er, ei = e_ref[j:j + 1, 0 * ns:1 * ns], e_ref[j:j + 1, 1 * ns:2 * ns]
        gr, gi = p_fr * gr - p_fi * gi + er, p_fr * gi + p_fi * gr + ei
        g_ref[j + 1:j + 2, 0 * ns:1 * ns] = gr
        g_ref[j + 1:j + 2, 1 * ns:2 * ns] = gi
    gr, gi = zrow, zrow
    for j in range(NSEG - 1, 0, -1):
        er, ei = e_ref[j:j + 1, 2 * ns:3 * ns], e_ref[j:j + 1, 3 * ns:4 * ns]
        gr, gi = p_br * gr - p_bi * gi + er, p_br * gi + p_bi * gr + ei
        g_ref[j - 1:j, 2 * ns:3 * ns] = gr
        g_ref[j - 1:j, 3 * ns:4 * ns] = gi

    scan((g_ref[:, 0 * ns:1 * ns], g_ref[:, 1 * ns:2 * ns],
          g_ref[:, 2 * ns:3 * ns], g_ref[:, 3 * ns:4 * ns]), True)

    y = jnp.dot(ucat, m_ref[...], preferred_element_type=F32)
    y = y + lax.dot_general(h_ref[...], wout_ref[...], (((1,), (1,)), ((), ())),
                            preferred_element_type=F32)
    for t in range(TCH):
        o_ref[:, t] = y[:, t * LANES:(t + 1) * LANES].reshape(nq, NSEG, LANES)


def _s5_mix(proj, col0, m, win, wout, av, layer, width):
    b, s, n = proj.shape
    ncb = width // LANES
    nq = s // (NSEG * TCH)
    nc = nq * NSEG
    kt = TCH * LANES
    nst = 16 * LANES
    proj5 = proj.reshape(b, nq, TCH, NSEG, n)
    out = pl.pallas_call(
        functools.partial(_s5_body, nq=nq),
        grid=(ncb, b),
        in_specs=[
            pl.BlockSpec((None, nq, TCH, NSEG, LANES), lambda c, bi: (bi, 0, 0, 0, col0 + c)),
            _resident((None, None, kt, kt), lambda c, bi: (layer, c, 0, 0)),
            _resident((None, None, kt, nst), lambda c, bi: (layer, c, 0, 0)),
            _resident((None, None, kt, nst), lambda c, bi: (layer, c, 0, 0)),
            pl.BlockSpec((None, None, 8, 4 * LANES), lambda c, bi: (layer, c, 0, 0)),
        ],
        out_specs=pl.BlockSpec((None, nq, TCH, NSEG, LANES), lambda c, bi: (bi, 0, 0, 0, c)),
        out_shape=jax.ShapeDtypeStruct((b, nq, TCH, NSEG, width), F32),
        scratch_shapes=[
            pltpu.VMEM((nc, kt), BF16),
            pltpu.VMEM((nc, nst), F32),
            pltpu.VMEM((nc, nst), BF16),
            pltpu.VMEM((NSEG, nst), F32),
            pltpu.VMEM((NSEG, nst), F32),
        ],
        compiler_params=_cparams(("parallel", "parallel")),
        name="s5_mix",
    )(proj5, m, win, wout, av)
    return out.reshape(b, s, width)


def _rglru_body(xr_ref, gr_ref, cw_ref, par_ref, wg_ref, o_ref,
                pad_ref, tmp_ref, af_ref, bf_ref, ab_ref, bb_ref, *, s, tt):
    r = NSEG
    ls = s // r
    pad_ref[2 * r:2 * r + s, :] = xr_ref[...].astype(F32)
    tmp_ref[...] = jnp.zeros((3 * r, LANES), F32)
    tmp_ref[r:2 * r, :] = xr_ref[s - 2 * r:s - r, :].astype(F32)
    pad_ref[0:r, :] = tmp_ref[r - 1:2 * r - 1, :]
    tmp_ref[r:2 * r, :] = xr_ref[s - r:s, :].astype(F32)
    pad_ref[r:2 * r, :] = tmp_ref[r - 1:2 * r - 1, :]
    tmp_ref[r:2 * r, :] = xr_ref[0:r, :].astype(F32)
    pad_ref[2 * r + s:3 * r + s, :] = tmp_ref[r + 1:2 * r + 1, :]

    cb = par_ref[0:1, :]
    hb_af, hb_xf, hb_ab, hb_xb = (0.5 * par_ref[k:k + 1, :] for k in range(1, 5))
    c2_f = (-0.5 * LRU_C * LOG2_E) * jax.nn.softplus(-par_ref[5:6, :])
    c2_b = (-0.5 * LRU_C * LOG2_E) * jax.nn.softplus(-par_ref[6:7, :])

    def gates(k, _):
        r0 = pl.multiple_of(k * tt, tt)
        xc = cb + cw_ref[0:1, :] * pad_ref[pl.ds(r0, tt), :]
        for tap in range(1, 4):
            xc = xc + cw_ref[tap:tap + 1, :] * pad_ref[pl.ds(r0 + tap * r, tt), :]
        z = jnp.dot(xc.astype(BF16), wg_ref[...], preferred_element_type=F32)
        hx = 0.5 * xc
        for (a_ref, b_ref, c2, hba, hbx, off) in ((af_ref, bf_ref, c2_f, hb_af, hb_xf, 0),
                                                  (ab_ref, bb_ref, c2_b, hb_ab, hb_xb, 2)):
            tr = jnp.tanh(z[:, off * LANES:(off + 1) * LANES] + hba)
            tg = jnp.tanh(z[:, (off + 1) * LANES:(off + 2) * LANES] + hbx)
            a = jnp.exp2(c2 * tr + c2)
            v = 1.0 - a * a
            root = jnp.where(v > 0.0, v * lax.rsqrt(v), 0.0)
            a_ref[pl.ds(r0, tt), :] = a
            b_ref[pl.ds(r0, tt), :] = root * (hx * tg + hx)
        return 0

    lax.fori_loop(0, s // tt, gates, 0)

    def rows(i):
        return pl.ds(pl.multiple_of(i * r, r), r)

    def local(i, carry):
        hf, pf, hb, pb = carry
        ib = ls - 1 - i
        a_f, a_b = af_ref[rows(i), :], ab_ref[rows(ib), :]
        return (a_f * hf + bf_ref[rows(i), :], a_f * pf, a_b * hb + bb_ref[rows(ib), :], a_b * pb)

    zero = jnp.zeros((r, LANES), F32)
    one = jnp.ones((r, LANES), F32)
    ef, pf, eb, pb = lax.fori_loop(0, ls, local, (zero, one, zero, one), unroll=SCAN_UNROLL)

    tmp_ref[0:r, :] = ef
    tmp_ref[r:2 * r, :] = pf
    g = jnp.zeros((1, LANES), F32)
    pad_ref[0:1, :] = g
    for j in range(r - 1):
        g = tmp_ref[r + j:r + j + 1, :] * g + tmp_ref[j:j + 1, :]
        pad_ref[j + 1:j + 2, :] = g
    gf = pad_ref[0:r, :]
    tmp_ref[0:r, :] = eb
    tmp_ref[r:2 * r, :] = pb
    g = jnp.zeros((1, LANES), F32)
    pad_ref[r - 1:r, :] = g
    for j in range(r - 1, 0, -1):
        g = tmp_ref[r + j:r + j + 1, :] * g + tmp_ref[j:j + 1, :]
        pad_ref[j - 1:j, :] = g
    gb = pad_ref[0:r, :]

    def both(i, carry):
        hf, hb = carry
        ib = ls - 1 - i
        hf = af_ref[rows(i), :] * hf + bf_ref[rows(i), :]
        bf_ref[rows(i), :] = hf
        hb = ab_ref[rows(ib), :] * hb + bb_ref[rows(ib), :]
        bb_ref[rows(ib), :] = hb
        return hf, hb

    lax.fori_loop(0, ls, both, (gf, gb), unroll=SCAN_UNROLL)

    def combine(k, _):
        rk = pl.ds(pl.multiple_of(k * tt, tt), tt)
        gate = jax.nn.gelu(gr_ref[rk, :].astype(F32))
        o_ref[rk, :] = ((bf_ref[rk, :] + bb_ref[rk, :]) * gate).astype(o_ref.dtype)
        return 0

    lax.fori_loop(0, s // tt, combine, 0)


def _rglru(proj, col_x, col_g, conv_w, par, wg, layer, width):
    b, s, n = proj.shape
    ncb = width // LANES
    tt = min(512, s)
    return pl.pallas_call(
        functools.partial(_rglru_body, s=s, tt=tt),
        grid=(ncb, b),
        in_specs=[
            pl.BlockSpec((None, s, LANES), lambda c, bi: (bi, 0, col_x + c)),
            pl.BlockSpec((None, s, LANES), lambda c, bi: (bi, 0, col_g + c)),
            pl.BlockSpec((None, 4, LANES), lambda c, bi: (layer, 0, c)),
            pl.BlockSpec((None, None, 8, LANES), lambda c, bi: (layer, c, 0, 0)),
            pl.BlockSpec((None, None, LANES, 4 * LANES), lambda c, bi: (layer, c, 0, 0)),
        ],
        out_specs=pl.BlockSpec((None, s, LANES), lambda c, bi: (bi, 0, c)),
        out_shape=jax.ShapeDtypeStruct((b, s, width), BF16),
        scratch_shapes=[
            pltpu.VMEM((s + 3 * NSEG, LANES), F32),
            pltpu.VMEM((3 * NSEG, LANES), F32),
            pltpu.VMEM((s, LANES), F32),
            pltpu.VMEM((s, LANES), F32),
            pltpu.VMEM((s, LANES), F32),
            pltpu.VMEM((s, LANES), F32),
        ],
        compiler_params=_cparams(("parallel", "parallel")),
        name="rglru",
    )(proj, proj, conv_w, par, wg)


def _postmix_body(x_ref, ys_ref, u_ref, yb_ref, ga0_ref, ga1_ref, gb0_ref, gb1_ref, d_ref,
                  wglu_ref, wa_ref, wb_ref, wo_ref, o_ref):
    y = jax.nn.gelu(ys_ref[...] + d_ref[...] * u_ref[...].astype(F32))
    z = jnp.dot(y.astype(BF16), wglu_ref[...], preferred_element_type=F32)
    ya = (y * _sigmoid(z)).astype(BF16)
    pa = jnp.dot(ya, wa_ref[...], preferred_element_type=F32)
    pb = jnp.dot(yb_ref[...], wb_ref[...], preferred_element_type=F32)
    half = ga0_ref.shape[-1]
    merged = [(_sigmoid(ga_ref[...].astype(F32)) * pa[:, k * half:(k + 1) * half]
               + _sigmoid(gb_ref[...].astype(F32)) * pb[:, k * half:(k + 1) * half]).astype(BF16)
              for k, (ga_ref, gb_ref) in enumerate(((ga0_ref, gb0_ref), (ga1_ref, gb1_ref)))]
    o_ref[...] = x_ref[...] + jnp.dot(jnp.concatenate(merged, axis=1), wo_ref[...],
                                      preferred_element_type=F32)


def _postmix(x, ys, proj, yb, d, wglu, wa, wb, wo, layer, tm):
    b, s, dm = x.shape
    w5 = ys.shape[-1]
    wl = yb.shape[-1]
    tm = min(tm, s)
    half = dm // 2
    assert w5 == half and wl == half and proj.shape[-1] == 7 * half
    row = lambda bi, i: (bi, i, 0)
    gate = lambda k: pl.BlockSpec((None, tm, half), lambda bi, i: (bi, i, k))
    return pl.pallas_call(
        _postmix_body,
        grid=(b, s // tm),
        in_specs=[
            pl.BlockSpec((None, tm, dm), row),
            pl.BlockSpec((None, tm, w5), row),
            pl.BlockSpec((None, tm, w5), row),
            pl.BlockSpec((None, tm, wl), row),
            gate(3), gate(4), gate(5), gate(6),
            pl.BlockSpec((None, 1, w5), lambda bi, i: (layer, 0, 0)),
            _resident((None, w5, w5), lambda bi, i: (layer, 0, 0)),
            _resident((None, w5, dm), lambda bi, i: (layer, 0, 0)),
            _resident((None, wl, dm), lambda bi, i: (layer, 0, 0)),
            _resident((None, dm, dm), lambda bi, i: (layer, 0, 0)),
        ],
        out_specs=pl.BlockSpec((None, tm, dm), row),
        out_shape=jax.ShapeDtypeStruct((b, s, dm), F32),
        compiler_params=_cparams(("parallel", "parallel")),
        name="postmix",
    )(x, ys, proj, yb, proj, proj, proj, proj, d, wglu, wa, wb, wo)


def _ffn_body(x_ref, xp_ref, xq_ref, g_ref, wug_ref, wuv_ref, cwg_ref, cwv_ref, wd_ref, o_ref,
              xn_ref, edge_ref, *, tm):
    r = NSEG
    i = pl.program_id(1)

    @pl.when(pl.program_id(2) == 0)
    def _():
        g = g_ref[...]
        xn_ref[r:r + tm, :] = _rms(x_ref[...], g).astype(BF16)
        edge_ref[...] = jnp.zeros(edge_ref.shape, F32)
        prev = _rms(xp_ref[...], g)
        edge_ref[r:2 * r, :] = prev
        prev = jnp.where(i == 0, edge_ref[r - 1:2 * r - 1, :], prev)
        xn_ref[0:r, :] = prev.astype(BF16)
        nxt = _rms(xq_ref[...], g)
        edge_ref[r:2 * r, :] = nxt
        nxt = jnp.where(i == pl.num_programs(1) - 1, edge_ref[r + 1:2 * r + 1, :], nxt)
        xn_ref[r + tm:2 * r + tm, :] = nxt.astype(BF16)
        o_ref[...] = x_ref[...]

    xn = xn_ref[...]

    def conv(w_ref, cw_ref):
        h = jnp.dot(xn, w_ref[...], preferred_element_type=F32)
        return (cw_ref[0:1, :] * h[0:tm] + cw_ref[1:2, :] * h[r:r + tm]
                + cw_ref[2:3, :] * h[2 * r:2 * r + tm])

    act = (jax.nn.gelu(conv(wug_ref, cwg_ref)) * conv(wuv_ref, cwv_ref)).astype(BF16)
    o_ref[...] += jnp.dot(act, wd_ref[...], preferred_element_type=F32)


def _ffn(x, g, wup, cw, wdown, layer, tm, tf):
    b, s, dm = x.shape
    fh = wdown.shape[1]
    tm = min(tm, s)
    tf = min(tf, fh)
    nf = fh // tf
    gpt = tm // NSEG
    ng = s // NSEG
    return pl.pallas_call(
        functools.partial(_ffn_body, tm=tm),
        grid=(b, s // tm, nf),
        in_specs=[
            pl.BlockSpec((None, tm, dm), lambda bi, i, f: (bi, i, 0)),
            pl.BlockSpec((None, NSEG, dm), lambda bi, i, f: (bi, lax.rem(i * gpt + ng - 1, ng), 0)),
            pl.BlockSpec((None, NSEG, dm), lambda bi, i, f: (bi, lax.rem((i + 1) * gpt, ng), 0)),
            pl.BlockSpec((None, 1, dm), lambda bi, i, f: (layer, 0, 0)),
            pl.BlockSpec((None, dm, tf), lambda bi, i, f: (layer, 0, f)),
            pl.BlockSpec((None, dm, tf), lambda bi, i, f: (layer, 0, nf + f)),
            pl.BlockSpec((None, 3, tf), lambda bi, i, f: (layer, 0, f)),
            pl.BlockSpec((None, 3, tf), lambda bi, i, f: (layer, 0, nf + f)),
            pl.BlockSpec((None, tf, dm), lambda bi, i, f: (layer, f, 0)),
        ],
        out_specs=pl.BlockSpec((None, tm, dm), lambda bi, i, f: (bi, i, 0)),
        out_shape=jax.ShapeDtypeStruct((b, s, dm), F32),
        scratch_shapes=[pltpu.VMEM((tm + 2 * NSEG, dm), BF16), pltpu.VMEM((3 * NSEG, dm), F32)],
        compiler_params=_cparams(("parallel", "parallel", "arbitrary")),
        name="ffn",
    )(x, x, x, g, wup, wup, cw, cw, wdown)


def _final_body(x_ref, g_ref, o_ref):
    o_ref[...] = _rms(x_ref[...], g_ref[...])


def _final_norm(x, g, tm):
    b, s, dm = x.shape
    tm = min(tm, s)
    return pl.pallas_call(
        _final_body,
        grid=(b, s // tm),
        in_specs=[pl.BlockSpec((None, tm, dm), lambda bi, i: (bi, i, 0)),
                  pl.BlockSpec((1, dm), lambda bi, i: (0, 0))],
        out_specs=pl.BlockSpec((None, tm, dm), lambda bi, i: (bi, i, 0)),
        out_shape=jax.ShapeDtypeStruct((b, s, dm), F32),
        compiler_params=_cparams(("parallel", "parallel")),
        name="final_norm",
    )(x, g)


def _block_diag_embed(w, nblk, spec_in, spec_out):
    return jnp.einsum(spec_in + ",gh->" + spec_out, w, jnp.eye(nblk, dtype=w.dtype))


def _s5_prep_body(tab_ref, bb_ref, cc_ref, m_ref, win_ref, woutt_ref, p_ref, q_ref, d_ref, *, npz):
    t = TCH
    ns = tab_ref.shape[-1]
    lane = lax.broadcasted_iota(jnp.int32, (S5_H, 2 * ns), 1)
    grp = (lane % ns) // npz
    nt = (((1,), (1,)), ((), ()))

    def spread(dst, compact):
        for g in range(GROUPS_PER_BLOCK):
            dst[g * S5_H:(g + 1) * S5_H, :] = jnp.where(grp == g, compact, 0.0)

    for d in range(2):
        br, bi = bb_ref[d, 0], bb_ref[d, 1]
        cr, ci = cc_ref[d, 0], cc_ref[d, 1]
        for k in range(t + 1):
            ar, ai = tab_ref[d, 0, k:k + 1, :], tab_ref[d, 1, k:k + 1, :]
            if k < t:
                spread(p_ref.at[d, k], jnp.concatenate([ar * br - ai * bi, ar * bi + ai * br], axis=1))
            spread(q_ref.at[d, k], jnp.concatenate([cr * ar - ci * ai, -(cr * ai + ci * ar)], axis=1))
        for k in range(t):
            d_ref[d, k] = lax.dot_general(p_ref[d, k], q_ref[d, 0], nt, precision=lax.Precision.HIGHEST,
                                          preferred_element_type=F32)

    for s in range(t):
        rows = slice(s * LANES, (s + 1) * LANES)
        for to in range(t):
            if to > s:
                blk = d_ref[0, to - s]
            elif to < s:
                blk = d_ref[1, s - to]
            else:
                blk = d_ref[0, 0] + d_ref[1, 0]
            m_ref[rows, to * LANES:(to + 1) * LANES] = blk.astype(BF16)
        win_ref[rows, 0:2 * ns] = p_ref[0, t - 1 - s].astype(BF16)
        win_ref[rows, 2 * ns:4 * ns] = p_ref[1, s].astype(BF16)
        woutt_ref[rows, 0:2 * ns] = q_ref[0, s + 1].astype(BF16)
        woutt_ref[rows, 2 * ns:4 * ns] = q_ref[1, t - s].astype(BF16)


def _prep_s5(a_re, a_im, log_dt, b_re, b_im, c_re, c_im, seg_lens):
    nl, _, ng, npz = a_re.shape
    gb = GROUPS_PER_BLOCK
    ncb = ng // gb
    ns = gb * npz
    kt = TCH * LANES
    lr, li = a_re.astype(F32), a_im.astype(F32)
    dt = jnp.exp(log_dt.astype(F32))[..., None]
    zr, zi = lr * dt, li * dt

    def cpow(k, re=zr, im=zi):
        mag = jnp.exp(re * k)
        return mag * jnp.cos(im * k), mag * jnp.sin(im * k)

    kv = jnp.arange(S5_H, dtype=F32)[:, None]
    tab = jnp.stack(cpow(kv, zr[..., None, :], zi[..., None, :]), axis=2)
    ar, ai = cpow(1.0)
    den = lr * lr + li * li
    fr = ((ar - 1.0) * lr + ai * li) / den
    fi = (ai * lr - (ar - 1.0) * li) / den
    b_re, b_im = b_re.astype(F32), b_im.astype(F32)
    bbr = fr[..., None] * b_re - fi[..., None] * b_im
    bbi = fr[..., None] * b_im + fi[..., None] * b_re
    bb = jnp.swapaxes(jnp.stack([bbr, bbi], axis=2), -1, -2)
    cc = jnp.stack([c_re.astype(F32), c_im.astype(F32)], axis=2)

    def lanes(v):
        v = v.reshape(nl, 2, 2, ncb, gb, S5_H, npz).transpose(0, 3, 1, 2, 5, 4, 6)
        return v.reshape(nl, ncb, 2, 2, S5_H, ns)

    tab_spec = pl.BlockSpec((None, None, 2, 2, S5_H, ns), lambda l, c: (l, c, 0, 0, 0, 0))
    w_spec = pl.BlockSpec((None, None, kt, 4 * ns), lambda l, c: (l, c, 0, 0))
    m, win, woutt = pl.pallas_call(
        functools.partial(_s5_prep_body, npz=npz),
        grid=(nl, ncb),
        in_specs=[tab_spec, tab_spec, tab_spec],
        out_specs=[pl.BlockSpec((None, None, kt, kt), lambda l, c: (l, c, 0, 0)), w_spec, w_spec],
        out_shape=[jax.ShapeDtypeStruct((nl, ncb, kt, kt), BF16),
                   jax.ShapeDtypeStruct((nl, ncb, kt, 4 * ns), BF16),
                   jax.ShapeDtypeStruct((nl, ncb, kt, 4 * ns), BF16)],
        scratch_shapes=[pltpu.VMEM((2, TCH, LANES, 2 * ns), F32),
                        pltpu.VMEM((2, TCH + 1, LANES, 2 * ns), F32),
                        pltpu.VMEM((2, TCH, LANES, LANES), F32)],
        compiler_params=_cparams(("parallel", "parallel")),
        name="s5_prep",
    )(lanes(tab), lanes(bb), lanes(cc))

    def vec(vr, vi):
        v = jnp.stack([vr[:, 0], vi[:, 0], vr[:, 1], vi[:, 1]], axis=1)
        return v.reshape(nl, 4, ncb, ns).transpose(0, 2, 1, 3)

    avs = [jnp.concatenate([vec(*cpow(float(TCH))), vec(*cpow(float(ls)))], axis=2) for ls in seg_lens]
    return m, win, woutt, avs


def _prep_lru(conv_b, w_a, b_a, w_x, b_x, lam):
    nl, _, nb, blk, _ = w_a.shape
    width = nb * blk
    ncb = width // LANES
    per = LANES // blk

    def bd(w):
        w = w.reshape(nl, ncb, per, blk, blk)
        return _block_diag_embed(w, per, "lcgij", "lcgihj").reshape(nl, ncb, LANES, LANES)

    wg = jnp.concatenate([bd(w_a[:, 0]), bd(w_x[:, 0]), bd(w_a[:, 1]), bd(w_x[:, 1])], axis=-1)
    rows = [conv_b, b_a[:, 0], b_x[:, 0], b_a[:, 1], b_x[:, 1], lam[:, 0], lam[:, 1],
            jnp.zeros_like(conv_b)]
    par = jnp.stack(rows, axis=1).astype(F32)
    par = par.reshape(nl, 8, ncb, LANES).transpose(0, 2, 1, 3)
    return (0.5 * wg).astype(BF16), par


def kernel(x_prompt, x_sample, norm1_g, w_in, s5_a_re, s5_a_im, s5_log_dt, s5_b_re, s5_b_im, s5_c_re, s5_c_im, s5_d, s5_w_glu, lru_conv_w, lru_conv_b, lru_w_a, lru_b_a, lru_w_x, lru_b_x, lru_lambda, w_proj_a, w_proj_b, w_out, norm2_g, ffn_w_up, ffn_conv_w, ffn_w_down, final_g):
    depth, dm, _ = w_in.shape
    w5 = s5_d.shape[-1]
    wl = lru_conv_b.shape[-1]
    groups = (x_prompt, x_sample)
    seg_lens = [x.shape[1] // NSEG for x in groups]

    w_in_p = w_in.astype(BF16)
    col_u = 0
    col_x = w5 // LANES
    col_g = col_x + wl // LANES
    m, win, wout, avs = _prep_s5(s5_a_re, s5_a_im, s5_log_dt, s5_b_re, s5_b_im, s5_c_re, s5_c_im, seg_lens)
    wg, par = _prep_lru(lru_conv_b, lru_w_a, lru_b_a, lru_w_x, lru_b_x, lru_lambda)
    conv_w = lru_conv_w.astype(F32)
    wglu, wa, wb, wo = (w.astype(BF16) for w in (s5_w_glu, w_proj_a, w_proj_b, w_out))
    wup, wdown = ffn_w_up.astype(BF16), ffn_w_down.astype(BF16)
    ffn_cw = ffn_conv_w.astype(F32)
    g1 = norm1_g.astype(F32)[:, None, :]
    g2 = norm2_g.astype(F32)[:, None, :]
    d5 = s5_d.astype(F32)[:, None, :]
    gfin = final_g.astype(F32)[None, :]

    outs = []
    for x, av in zip(groups, avs):
        x = _interleave(x.astype(F32))
        for l in range(depth):
            proj = _inproj(x, g1, w_in_p, l, tm=512, tn=min(1024, w5))
            ys = _s5_mix(proj, col_u, m, win, wout, av, l, w5)
            yb = _rglru(proj, col_x, col_g, conv_w, par, wg, l, wl)
            x = _postmix(x, ys, proj, yb, d5, wglu, wa, wb, wo, l, tm=256)
            x = _ffn(x, g2, wup, ffn_cw, wdown, l, tm=512, tf=512)
        outs.append(_deinterleave(_final_norm(x, gfin, tm=512)))
    return tuple(outs)
```

```python
import functools
import math

import jax
import jax.numpy as jnp
from jax import lax
from jax.experimental import pallas as pl
from jax.experimental.pallas import tpu as pltpu

F32 = jnp.float32
BF16 = jnp.bfloat16

LANES = 128
NSEG = 16
TCH = 8
S5_H = 16
GROUPS_PER_BLOCK = LANES // S5_H
EPS = 1e-6
LRU_C = 8.0
LOG2_E = 1.0 / math.log(2.0)
VMEM_LIMIT = 60 * 1024 * 1024


def _cparams(sem, vmem=VMEM_LIMIT):
    return pltpu.CompilerParams(dimension_semantics=sem, vmem_limit_bytes=vmem)


def _resident(shape, imap):
    return pl.BlockSpec(shape, imap, pipeline_mode=pl.Buffered(1))


def _interleave(x):
    b, s, d = x.shape
    return x.reshape(b, NSEG, s // NSEG, d).transpose(0, 2, 1, 3).reshape(b, s, d)


def _deinterleave(x):
    b, s, d = x.shape
    return x.reshape(b, s // NSEG, NSEG, d).transpose(0, 2, 1, 3).reshape(b, s, d)


SCAN_UNROLL = 8


def _sigmoid(x):
    return 0.5 * jnp.tanh(0.5 * x) + 0.5


def _rms(xf, g):
    return xf * lax.rsqrt(jnp.mean(xf * xf, axis=-1, keepdims=True) + EPS) * g


def _inproj_body(x_ref, g_ref, w_ref, o_ref, xn_ref):
    @pl.when(pl.program_id(2) == 0)
    def _():
        xn_ref[...] = _rms(x_ref[...], g_ref[...]).astype(BF16)

    o_ref[...] = jnp.dot(xn_ref[...], w_ref[...], preferred_element_type=F32).astype(o_ref.dtype)


def _inproj(x, g, w, layer, tm, tn):
    b, s, d = x.shape
    n = w.shape[-1]
    tm = min(tm, s)
    return pl.pallas_call(
        _inproj_body,
        grid=(b, s // tm, n // tn),
        in_specs=[
            pl.BlockSpec((None, tm, d), lambda bi, i, j: (bi, i, 0)),
            pl.BlockSpec((None, 1, d), lambda bi, i, j: (layer, 0, 0)),
            pl.BlockSpec((None, d, tn), lambda bi, i, j: (layer, 0, j)),
        ],
        out_specs=pl.BlockSpec((None, tm, tn), lambda bi, i, j: (bi, i, j)),
        out_shape=jax.ShapeDtypeStruct((b, s, n), BF16),
        scratch_shapes=[pltpu.VMEM((tm, d), BF16)],
        compiler_params=_cparams(("parallel", "parallel", "arbitrary")),
        name="inproj",
    )(x, g, w)


def _s5_body(u_ref, m_ref, win_ref, wout_ref, av_ref, o_ref,
             ucat_ref, s_ref, h_ref, e_ref, g_ref, *, nq):
    nc = nq * NSEG
    ns = 4 * LANES
    for t in range(TCH):
        ucat_ref[:, t * LANES:(t + 1) * LANES] = u_ref[:, t].reshape(nc, LANES)
    ucat = ucat_ref[...]
    s_ref[...] = jnp.dot(ucat, win_ref[...], preferred_element_type=F32)

    a_fr, a_fi = av_ref[0:1, :], av_ref[1:2, :]
    a_br, a_bi = av_ref[2:3, :], av_ref[3:4, :]

    def scan(init, store):
        def step(q, carry):
            hfr, hfi, hbr, hbi = carry
            rf = pl.multiple_of(q * NSEG, NSEG)
            rb = pl.multiple_of((nq - 1 - q) * NSEG, NSEG)
            if store:
                h_ref[pl.ds(rf, NSEG), 0 * ns:1 * ns] = hfr.astype(BF16)
                h_ref[pl.ds(rf, NSEG), 1 * ns:2 * ns] = hfi.astype(BF16)
                h_ref[pl.ds(rb, NSEG), 2 * ns:3 * ns] = hbr.astype(BF16)
                h_ref[pl.ds(rb, NSEG), 3 * ns:4 * ns] = hbi.astype(BF16)
            sfr = s_ref[pl.ds(rf, NSEG), 0 * ns:1 * ns]
            sfi = s_ref[pl.ds(rf, NSEG), 1 * ns:2 * ns]
            sbr = s_ref[pl.ds(rb, NSEG), 2 * ns:3 * ns]
            sbi = s_ref[pl.ds(rb, NSEG), 3 * ns:4 * ns]
            return (a_fr * hfr - a_fi * hfi + sfr, a_fr * hfi + a_fi * hfr + sfi,
                    a_br * hbr - a_bi * hbi + sbr, a_br * hbi + a_bi * hbr + sbi)
        return lax.fori_loop(0, nq, step, init)

    zero = jnp.zeros((NSEG, ns), F32)
    efr, efi, ebr, ebi = scan((zero, zero, zero, zero), False)
    e_ref[:, 0 * ns:1 * ns] = efr
    e_ref[:, 1 * ns:2 * ns] = efi
    e_ref[:, 2 * ns:3 * ns] = ebr
    e_ref[:, 3 * ns:4 * ns] = ebi

    p_fr, p_fi = av_ref[4:5, :], av_ref[5:6, :]
    p_br, p_bi = av_ref[6:7, :], av_ref[7:8, :]
    zrow = jnp.zeros((1, ns), F32)
    g_ref[0:1, 0:2 * ns] = jnp.zeros((1, 2 * ns), F32)
    g_ref[NSEG - 1:NSEG, 2 * ns:4 * ns] = jnp.zeros((1, 2 * ns), F32)
    gr, gi = zrow, zrow
    for j in range(NSEG - 1):
        er, ei = e_ref[j:j + 1, 0 * ns:1 * ns], e_ref[j:j + 1, 1 * ns:2 * ns]
        gr, gi = p_fr * gr - p_fi * gi + er, p_fr * gi + p_fi * gr + ei
        g_ref[j + 1:j + 2, 0 * ns:1 * ns] = gr
        g_ref[j + 1:j + 2, 1 * ns:2 * ns] = gi
    gr, gi = zrow, zrow
    for j in range(NSEG - 1, 0, -1):
        er, ei = e_ref[j:j + 1, 2 * ns:3 * ns], e_ref[j:j + 1, 3 * ns:4 * ns]
        gr, gi = p_br * gr - p_bi * gi + er, p_br * gi + p_bi * gr + ei
        g_ref[j - 1:j, 2 * ns:3 * ns] = gr
        g_ref[j - 1:j, 3 * ns:4 * ns] = gi

    scan((g_ref[:, 0 * ns:1 * ns], g_ref[:, 1 * ns:2 * ns],
          g_ref[:, 2 * ns:3 * ns], g_ref[:, 3 * ns:4 * ns]), True)

    y = jnp.dot(ucat, m_ref[...], preferred_element_type=F32)
    y = y + lax.dot_general(h_ref[...], wout_ref[...], (((1,), (1,)), ((), ())),
                            preferred_element_type=F32)
    for t in range(TCH):
        o_ref[:, t] = y[:, t * LANES:(t + 1) * LANES].reshape(nq, NSEG, LANES).astype(o_ref.dtype)


def _s5_mix(proj, col0, m, win, wout, av, layer, width):
    b, s, n = proj.shape
    ncb = width // LANES
    nq = s // (NSEG * TCH)
    nc = nq * NSEG
    kt = TCH * LANES
    nst = 16 * LANES
    proj5 = proj.reshape(b, nq, TCH, NSEG, n)
    out = pl.pallas_call(
        functools.partial(_s5_body, nq=nq),
        grid=(ncb, b),
        in_specs=[
            pl.BlockSpec((None, nq, TCH, NSEG, LANES), lambda c, bi: (bi, 0, 0, 0, col0 + c)),
            _resident((None, None, kt, kt), lambda c, bi: (layer, c, 0, 0)),
            _resident((None, None, kt, nst), lambda c, bi: (layer, c, 0, 0)),
            _resident((None, None, kt, nst), lambda c, bi: (layer, c, 0, 0)),
            pl.BlockSpec((None, None, 8, 4 * LANES), lambda c, bi: (layer, c, 0, 0)),
        ],
        out_specs=pl.BlockSpec((None, nq, TCH, NSEG, LANES), lambda c, bi: (bi, 0, 0, 0, c)),
        out_shape=jax.ShapeDtypeStruct((b, nq, TCH, NSEG, width), BF16),
        scratch_shapes=[
            pltpu.VMEM((nc, kt), BF16),
            pltpu.VMEM((nc, nst), F32),
            pltpu.VMEM((nc, nst), BF16),
            pltpu.VMEM((NSEG, nst), F32),
            pltpu.VMEM((NSEG, nst), F32),
        ],
        compiler_params=_cparams(("parallel", "parallel")),
        name="s5_mix",
    )(proj5, m, win, wout, av)
    return out.reshape(b, s, width)


def _rglru_body(xr_ref, gr_ref, cw_ref, par_ref, wg_ref, o_ref,
                pad_ref, tmp_ref, af_ref, bf_ref, ab_ref, bb_ref, *, s, tt):
    r = NSEG
    ls = s // r
    pad_ref[2 * r:2 * r + s, :] = xr_ref[...].astype(F32)
    tmp_ref[...] = jnp.zeros((3 * r, LANES), F32)
    tmp_ref[r:2 * r, :] = xr_ref[s - 2 * r:s - r, :].astype(F32)
    pad_ref[0:r, :] = tmp_ref[r - 1:2 * r - 1, :]
    tmp_ref[r:2 * r, :] = xr_ref[s - r:s, :].astype(F32)
    pad_ref[r:2 * r, :] = tmp_ref[r - 1:2 * r - 1, :]
    tmp_ref[r:2 * r, :] = xr_ref[0:r, :].astype(F32)
    pad_ref[2 * r + s:3 * r + s, :] = tmp_ref[r + 1:2 * r + 1, :]

    cb = par_ref[0:1, :]
    hb_af, hb_xf, hb_ab, hb_xb = (0.5 * par_ref[k:k + 1, :] for k in range(1, 5))
    c2_f = (-0.5 * LRU_C * LOG2_E) * jax.nn.softplus(-par_ref[5:6, :])
    c2_b = (-0.5 * LRU_C * LOG2_E) * jax.nn.softplus(-par_ref[6:7, :])

    def gates(k, _):
        r0 = pl.multiple_of(k * tt, tt)
        xc = cb + cw_ref[0:1, :] * pad_ref[pl.ds(r0, tt), :]
        for tap in range(1, 4):
            xc = xc + cw_ref[tap:tap + 1, :] * pad_ref[pl.ds(r0 + tap * r, tt), :]
        z = jnp.dot(xc.astype(BF16), wg_ref[...], preferred_element_type=F32)
        hx = 0.5 * xc
        for (a_ref, b_ref, c2, hba, hbx, off) in ((af_ref, bf_ref, c2_f, hb_af, hb_xf, 0),
                                                  (ab_ref, bb_ref, c2_b, hb_ab, hb_xb, 2)):
            tr = jnp.tanh(z[:, off * LANES:(off + 1) * LANES] + hba)
            tg = jnp.tanh(z[:, (off + 1) * LANES:(off + 2) * LANES] + hbx)
            a = jnp.exp2(c2 * tr + c2)
            v = 1.0 - a * a
            root = jnp.where(v > 0.0, v * lax.rsqrt(v), 0.0)
            a_ref[pl.ds(r0, tt), :] = a
            b_ref[pl.ds(r0, tt), :] = root * (hx * tg + hx)
        return 0

    lax.fori_loop(0, s // tt, gates, 0)

    def rows(i):
        return pl.ds(pl.multiple_of(i * r, r), r)

    def local(i, carry):
        hf, pf, hb, pb = carry
        ib = ls - 1 - i
        a_f, a_b = af_ref[rows(i), :], ab_ref[rows(ib), :]
        return (a_f * hf + bf_ref[rows(i), :], a_f * pf, a_b * hb + bb_ref[rows(ib), :], a_b * pb)

    zero = jnp.zeros((r, LANES), F32)
    one = jnp.ones((r, LANES), F32)
    ef, pf, eb, pb = lax.fori_loop(0, ls, local, (zero, one, zero, one), unroll=SCAN_UNROLL)

    tmp_ref[0:r, :] = ef
    tmp_ref[r:2 * r, :] = pf
    g = jnp.zeros((1, LANES), F32)
    pad_ref[0:1, :] = g
    for j in range(r - 1):
        g = tmp_ref[r + j:r + j + 1, :] * g + tmp_ref[j:j + 1, :]
        pad_ref[j + 1:j + 2, :] = g
    gf = pad_ref[0:r, :]
    tmp_ref[0:r, :] = eb
    tmp_ref[r:2 * r, :] = pb
    g = jnp.zeros((1, LANES), F32)
    pad_ref[r - 1:r, :] = g
    for j in range(r - 1, 0, -1):
        g = tmp_ref[r + j:r + j + 1, :] * g + tmp_ref[j:j + 1, :]
        pad_ref[j - 1:j, :] = g
    gb = pad_ref[0:r, :]

    def both(i, carry):
        hf, hb = carry
        ib = ls - 1 - i
        hf = af_ref[rows(i), :] * hf + bf_ref[rows(i), :]
        bf_ref[rows(i), :] = hf
        hb = ab_ref[rows(ib), :] * hb + bb_ref[rows(ib), :]
        bb_ref[rows(ib), :] = hb
        return hf, hb

    lax.fori_loop(0, ls, both, (gf, gb), unroll=SCAN_UNROLL)

    def combine(k, _):
        rk = pl.ds(pl.multiple_of(k * tt, tt), tt)
        gate = jax.nn.gelu(gr_ref[rk, :].astype(F32))
        o_ref[rk, :] = ((bf_ref[rk, :] + bb_ref[rk, :]) * gate).astype(o_ref.dtype)
        return 0

    lax.fori_loop(0, s // tt, combine, 0)


def _rglru(proj, col_x, col_g, conv_w, par, wg, layer, width):
    b, s, n = proj.shape
    ncb = width // LANES
    tt = min(512, s)
    return pl.pallas_call(
        functools.partial(_rglru_body, s=s, tt=tt),
        grid=(ncb, b),
        in_specs=[
            pl.BlockSpec((None, s, LANES), lambda c, bi: (bi, 0, col_x + c)),
            pl.BlockSpec((None, s, LANES), lambda c, bi: (bi, 0, col_g + c)),
            pl.BlockSpec((None, 4, LANES), lambda c, bi: (layer, 0, c)),
            pl.BlockSpec((None, None, 8, LANES), lambda c, bi: (layer, c, 0, 0)),
            pl.BlockSpec((None, None, LANES, 4 * LANES), lambda c, bi: (layer, c, 0, 0)),
        ],
        out_specs=pl.BlockSpec((None, s, LANES), lambda c, bi: (bi, 0, c)),
        out_shape=jax.ShapeDtypeStruct((b, s, width), BF16),
        scratch_shapes=[
            pltpu.VMEM((s + 3 * NSEG, LANES), F32),
            pltpu.VMEM((3 * NSEG, LANES), F32),
            pltpu.VMEM((s, LANES), F32),
            pltpu.VMEM((s, LANES), F32),
            pltpu.VMEM((s, LANES), F32),
            pltpu.VMEM((s, LANES), F32),
        ],
        compiler_params=_cparams(("parallel", "parallel")),
        name="rglru",
    )(proj, proj, conv_w, par, wg)


def _postmix_body(x_ref, ys_ref, u_ref, yb_ref, ga0_ref, ga1_ref, gb0_ref, gb1_ref, d_ref,
                  wglu_ref, wa_ref, wb_ref, wo_ref, o_ref):
    y = jax.nn.gelu(ys_ref[...].astype(F32) + d_ref[...] * u_ref[...].astype(F32))
    z = jnp.dot(y.astype(BF16), wglu_ref[...], preferred_element_type=F32)
    ya = (y * _sigmoid(z)).astype(BF16)
    pa = jnp.dot(ya, wa_ref[...], preferred_element_type=F32)
    pb = jnp.dot(yb_ref[...], wb_ref[...], preferred_element_type=F32)
    half = ga0_ref.shape[-1]
    merged = [(_sigmoid(ga_ref[...].astype(F32)) * pa[:, k * half:(k + 1) * half]
               + _sigmoid(gb_ref[...].astype(F32)) * pb[:, k * half:(k + 1) * half]).astype(BF16)
              for k, (ga_ref, gb_ref) in enumerate(((ga0_ref, gb0_ref), (ga1_ref, gb1_ref)))]
    o_ref[...] = x_ref[...] + jnp.dot(jnp.concatenate(merged, axis=1), wo_ref[...],
                                      preferred_element_type=F32)


def _postmix(x, ys, proj, yb, d, wglu, wa, wb, wo, layer, tm):
    b, s, dm = x.shape
    w5 = ys.shape[-1]
    wl = yb.shape[-1]
    tm = min(tm, s)
    half = dm // 2
    assert w5 == half and wl == half and proj.shape[-1] == 7 * half
    row = lambda bi, i: (bi, i, 0)
    gate = lambda k: pl.BlockSpec((None, tm, half), lambda bi, i: (bi, i, k))
    return pl.pallas_call(
        _postmix_body,
        grid=(b, s // tm),
        in_specs=[
            pl.BlockSpec((None, tm, dm), row),
            pl.BlockSpec((None, tm, w5), row),
            pl.BlockSpec((None, tm, w5), row),
            pl.BlockSpec((None, tm, wl), row),
            gate(3), gate(4), gate(5), gate(6),
            pl.BlockSpec((None, 1, w5), lambda bi, i: (layer, 0, 0)),
            _resident((None, w5, w5), lambda bi, i: (layer, 0, 0)),
            _resident((None, w5, dm), lambda bi, i: (layer, 0, 0)),
            _resident((None, wl, dm), lambda bi, i: (layer, 0, 0)),
            _resident((None, dm, dm), lambda bi, i: (layer, 0, 0)),
        ],
        out_specs=pl.BlockSpec((None, tm, dm), row),
        out_shape=jax.ShapeDtypeStruct((b, s, dm), F32),
        compiler_params=_cparams(("parallel", "parallel")),
        name="postmix",
    )(x, ys, proj, yb, proj, proj, proj, proj, d, wglu, wa, wb, wo)


def _ffn_body(x_ref, xp_ref, xq_ref, g_ref, wug_ref, wuv_ref, cwg_ref, cwv_ref, wd_ref, o_ref,
              xn_ref, edge_ref, *, tm):
    r = NSEG
    i = pl.program_id(1)

    @pl.when(pl.program_id(2) == 0)
    def _():
        g = g_ref[...]
        xn_ref[r:r + tm, :] = _rms(x_ref[...], g).astype(BF16)
        edge_ref[...] = jnp.zeros(edge_ref.shape, F32)
        prev = _rms(xp_ref[...], g)
        edge_ref[r:2 * r, :] = prev
        prev = jnp.where(i == 0, edge_ref[r - 1:2 * r - 1, :], prev)
        xn_ref[0:r, :] = prev.astype(BF16)
        nxt = _rms(xq_ref[...], g)
        edge_ref[r:2 * r, :] = nxt
        nxt = jnp.where(i == pl.num_programs(1) - 1, edge_ref[r + 1:2 * r + 1, :], nxt)
        xn_ref[r + tm:2 * r + tm, :] = nxt.astype(BF16)
        o_ref[...] = x_ref[...]

    xn = xn_ref[...]

    def conv(w_ref, cw_ref):
        h = jnp.dot(xn, w_ref[...], preferred_element_type=F32)
        return (cw_ref[0:1, :] * h[0:tm] + cw_ref[1:2, :] * h[r:r + tm]
                + cw_ref[2:3, :] * h[2 * r:2 * r + tm])

    act = (jax.nn.gelu(conv(wug_ref, cwg_ref)) * conv(wuv_ref, cwv_ref)).astype(BF16)
    o_ref[...] += jnp.dot(act, wd_ref[...], preferred_element_type=F32)


def _ffn(x, g, wup, cw, wdown, layer, tm, tf):
    b, s, dm = x.shape
    fh = wdown.shape[1]
    tm = min(tm, s)
    tf = min(tf, fh)
    nf = fh // tf
    gpt = tm // NSEG
    ng = s // NSEG
    return pl.pallas_call(
        functools.partial(_ffn_body, tm=tm),
        grid=(b, s // tm, nf),
        in_specs=[
            _resident((None, tm, dm), lambda bi, i, f: (bi, i, 0)),
            pl.BlockSpec((None, NSEG, dm), lambda bi, i, f: (bi, lax.rem(i * gpt + ng - 1, ng), 0)),
            pl.BlockSpec((None, NSEG, dm), lambda bi, i, f: (bi, lax.rem((i + 1) * gpt, ng), 0)),
            pl.BlockSpec((None, 1, dm), lambda bi, i, f: (layer, 0, 0)),
            pl.BlockSpec((None, dm, tf), lambda bi, i, f: (layer, 0, f)),
            pl.BlockSpec((None, dm, tf), lambda bi, i, f: (layer, 0, nf + f)),
            pl.BlockSpec((None, 3, tf), lambda bi, i, f: (layer, 0, f)),
            pl.BlockSpec((None, 3, tf), lambda bi, i, f: (layer, 0, nf + f)),
            pl.BlockSpec((None, tf, dm), lambda bi, i, f: (layer, f, 0)),
        ],
        out_specs=pl.BlockSpec((None, tm, dm), lambda bi, i, f: (bi, i, 0)),
        out_shape=jax.ShapeDtypeStruct((b, s, dm), F32),
        scratch_shapes=[pltpu.VMEM((tm + 2 * NSEG, dm), BF16), pltpu.VMEM((3 * NSEG, dm), F32)],
        compiler_params=_cparams(("parallel", "parallel", "arbitrary")),
        name="ffn",
    )(x, x, x, g, wup, wup, cw, cw, wdown)


def _final_body(x_ref, g_ref, o_ref):
    o_ref[...] = _rms(x_ref[...], g_ref[...])


def _final_norm(x, g, tm):
    b, s, dm = x.shape
    tm = min(tm, s)
    return pl.pallas_call(
        _final_body,
        grid=(b, s // tm),
        in_specs=[pl.BlockSpec((None, tm, dm), lambda bi, i: (bi, i, 0)),
                  pl.BlockSpec((1, dm), lambda bi, i: (0, 0))],
        out_specs=pl.BlockSpec((None, tm, dm), lambda bi, i: (bi, i, 0)),
        out_shape=jax.ShapeDtypeStruct((b, s, dm), F32),
        compiler_params=_cparams(("parallel", "parallel")),
        name="final_norm",
    )(x, g)


def _block_diag_embed(w, nblk, spec_in, spec_out):
    return jnp.einsum(spec_in + ",gh->" + spec_out, w, jnp.eye(nblk, dtype=w.dtype))


def _s5_prep_body(tab_ref, bb_ref, cc_ref, m_ref, win_ref, woutt_ref, p_ref, q_ref, d_ref, *, npz):
    t = TCH
    ns = tab_ref.shape[-1]
    lane = lax.broadcasted_iota(jnp.int32, (S5_H, 2 * ns), 1)
    grp = (lane % ns) // npz
    nt = (((1,), (1,)), ((), ()))

    def spread(dst, compact):
        for g in range(GROUPS_PER_BLOCK):
            dst[g * S5_H:(g + 1) * S5_H, :] = jnp.where(grp == g, compact, 0.0)

    for d in range(2):
        br, bi = bb_ref[d, 0], bb_ref[d, 1]
        cr, ci = cc_ref[d, 0], cc_ref[d, 1]
        for k in range(t + 1):
            ar, ai = tab_ref[d, 0, k:k + 1, :], tab_ref[d, 1, k:k + 1, :]
            if k < t:
                spread(p_ref.at[d, k], jnp.concatenate([ar * br - ai * bi, ar * bi + ai * br], axis=1))
            spread(q_ref.at[d, k], jnp.concatenate([cr * ar - ci * ai, -(cr * ai + ci * ar)], axis=1))
        for k in range(t):
            pk, q0 = p_ref[d, k], q_ref[d, 0]
            ph, qh = pk.astype(BF16), q0.astype(BF16)
            pl_, ql = (pk - ph.astype(F32)).astype(BF16), (q0 - qh.astype(F32)).astype(BF16)
            d_ref[d, k] = (lax.dot_general(ph, qh, nt, preferred_element_type=F32)
                           + lax.dot_general(ph, ql, nt, preferred_element_type=F32)
                           + lax.dot_general(pl_, qh, nt, preferred_element_type=F32))

    for s in range(t):
        rows = slice(s * LANES, (s + 1) * LANES)
        for to in range(t):
            if to > s:
                blk = d_ref[0, to - s]
            elif to < s:
                blk = d_ref[1, s - to]
            else:
                blk = d_ref[0, 0] + d_ref[1, 0]
            m_ref[rows, to * LANES:(to + 1) * LANES] = blk.astype(BF16)
        win_ref[rows, 0:2 * ns] = p_ref[0, t - 1 - s].astype(BF16)
        win_ref[rows, 2 * ns:4 * ns] = p_ref[1, s].astype(BF16)
        woutt_ref[rows, 0:2 * ns] = q_ref[0, s + 1].astype(BF16)
        woutt_ref[rows, 2 * ns:4 * ns] = q_ref[1, t - s].astype(BF16)


def _prep_s5(a_re, a_im, log_dt, b_re, b_im, c_re, c_im, seg_lens):
    nl, _, ng, npz = a_re.shape
    gb = GROUPS_PER_BLOCK
    ncb = ng // gb
    ns = gb * npz
    kt = TCH * LANES
    lr, li = a_re.astype(F32), a_im.astype(F32)
    dt = jnp.exp(log_dt.astype(F32))[..., None]
    zr, zi = lr * dt, li * dt

    def cpow(k, re=zr, im=zi):
        mag = jnp.exp(re * k)
        return mag * jnp.cos(im * k), mag * jnp.sin(im * k)

    kv = jnp.arange(S5_H, dtype=F32)[:, None]
    tab = jnp.stack(cpow(kv, zr[..., None, :], zi[..., None, :]), axis=2)
    ar, ai = cpow(1.0)
    den = lr * lr + li * li
    fr = ((ar - 1.0) * lr + ai * li) / den
    fi = (ai * lr - (ar - 1.0) * li) / den
    b_re, b_im = b_re.astype(F32), b_im.astype(F32)
    bbr = fr[..., None] * b_re - fi[..., None] * b_im
    bbi = fr[..., None] * b_im + fi[..., None] * b_re
    bb = jnp.swapaxes(jnp.stack([bbr, bbi], axis=2), -1, -2)
    cc = jnp.stack([c_re.astype(F32), c_im.astype(F32)], axis=2)

    def lanes(v):
        v = v.reshape(nl, 2, 2, ncb, gb, S5_H, npz).transpose(0, 3, 1, 2, 5, 4, 6)
        return v.reshape(nl, ncb, 2, 2, S5_H, ns)

    tab_spec = pl.BlockSpec((None, None, 2, 2, S5_H, ns), lambda l, c: (l, c, 0, 0, 0, 0))
    w_spec = pl.BlockSpec((None, None, kt, 4 * ns), lambda l, c: (l, c, 0, 0))
    m, win, woutt = pl.pallas_call(
        functools.partial(_s5_prep_body, npz=npz),
        grid=(nl, ncb),
        in_specs=[tab_spec, tab_spec, tab_spec],
        out_specs=[pl.BlockSpec((None, None, kt, kt), lambda l, c: (l, c, 0, 0)), w_spec, w_spec],
        out_shape=[jax.ShapeDtypeStruct((nl, ncb, kt, kt), BF16),
                   jax.ShapeDtypeStruct((nl, ncb, kt, 4 * ns), BF16),
                   jax.ShapeDtypeStruct((nl, ncb, kt, 4 * ns), BF16)],
        scratch_shapes=[pltpu.VMEM((2, TCH, LANES, 2 * ns), F32),
                        pltpu.VMEM((2, TCH + 1, LANES, 2 * ns), F32),
                        pltpu.VMEM((2, TCH, LANES, LANES), F32)],
        compiler_params=_cparams(("parallel", "parallel")),
        name="s5_prep",
    )(lanes(tab), lanes(bb), lanes(cc))

    def vec(vr, vi):
        v = jnp.stack([vr[:, 0], vi[:, 0], vr[:, 1], vi[:, 1]], axis=1)
        return v.reshape(nl, 4, ncb, ns).transpose(0, 2, 1, 3)

    avs = [jnp.concatenate([vec(*cpow(float(TCH))), vec(*cpow(float(ls)))], axis=2) for ls in seg_lens]
    return m, win, woutt, avs


def _prep_lru(conv_b, w_a, b_a, w_x, b_x, lam):
    nl, _, nb, blk, _ = w_a.shape
    width = nb * blk
    ncb = width // LANES
    per = LANES // blk

    def bd(w):
        w = w.reshape(nl, ncb, per, blk, blk)
        return _block_diag_embed(w, per, "lcgij", "lcgihj").reshape(nl, ncb, LANES, LANES)

    wg = jnp.concatenate([bd(w_a[:, 0]), bd(w_x[:, 0]), bd(w_a[:, 1]), bd(w_x[:, 1])], axis=-1)
    rows = [conv_b, b_a[:, 0], b_x[:, 0], b_a[:, 1], b_x[:, 1], lam[:, 0], lam[:, 1],
            jnp.zeros_like(conv_b)]
    par = jnp.stack(rows, axis=1).astype(F32)
    par = par.reshape(nl, 8, ncb, LANES).transpose(0, 2, 1, 3)
    return (0.5 * wg).astype(BF16), par


def kernel(x_prompt, x_sample, norm1_g, w_in, s5_a_re, s5_a_im, s5_log_dt, s5_b_re, s5_b_im, s5_c_re, s5_c_im, s5_d, s5_w_glu, lru_conv_w, lru_conv_b, lru_w_a, lru_b_a, lru_w_x, lru_b_x, lru_lambda, w_proj_a, w_proj_b, w_out, norm2_g, ffn_w_up, ffn_conv_w, ffn_w_down, final_g):
    depth, dm, _ = w_in.shape
    w5 = s5_d.shape[-1]
    wl = lru_conv_b.shape[-1]
    groups = (x_prompt, x_sample)
    seg_lens = [x.shape[1] // NSEG for x in groups]

    w_in_p = w_in.astype(BF16)
    col_u = 0
    col_x = w5 // LANES
    col_g = col_x + wl // LANES
    m, win, wout, avs = _prep_s5(s5_a_re, s5_a_im, s5_log_dt, s5_b_re, s5_b_im, s5_c_re, s5_c_im, seg_lens)
    wg, par = _prep_lru(lru_conv_b, lru_w_a, lru_b_a, lru_w_x, lru_b_x, lru_lambda)
    conv_w = lru_conv_w.astype(F32)
    wglu, wa, wb, wo = (w.astype(BF16) for w in (s5_w_glu, w_proj_a, w_proj_b, w_out))
    wup, wdown = ffn_w_up.astype(BF16), ffn_w_down.astype(BF16)
    ffn_cw = ffn_conv_w.astype(F32)
    g1 = norm1_g.astype(F32)[:, None, :]
    g2 = norm2_g.astype(F32)[:, None, :]
    d5 = s5_d.astype(F32)[:, None, :]
    gfin = final_g.astype(F32)[None, :]

    outs = []
    for x, av in zip(groups, avs):
        x = _interleave(x.astype(F32))
        for l in range(depth):
            proj = _inproj(x, g1, w_in_p, l, tm=1024, tn=min(1024, w5))
            ys = _s5_mix(proj, col_u, m, win, wout, av, l, w5)
            yb = _rglru(proj, col_x, col_g, conv_w, par, wg, l, wl)
            x = _postmix(x, ys, proj, yb, d5, wglu, wa, wb, wo, l, tm=256)
            x = _ffn(x, g2, wup, ffn_cw, wdown, l, tm=1024, tf=512)
        outs.append(_deinterleave(_final_norm(x, gfin, tm=512)))
    return tuple(outs)
```

```python
import functools
import math

import jax
import jax.numpy as jnp
from jax import lax
from jax.experimental import pallas as pl
from jax.experimental.pallas import tpu as pltpu

F32 = jnp.float32
BF16 = jnp.bfloat16

LANES = 128
NSEG = 16
TCH = 8
S5_H = 16
GROUPS_PER_BLOCK = LANES // S5_H
EPS = 1e-6
LRU_C = 8.0
LOG2_E = 1.0 / math.log(2.0)
VMEM_LIMIT = 60 * 1024 * 1024


def _cparams(sem, vmem=VMEM_LIMIT):
    return pltpu.CompilerParams(dimension_semantics=sem, vmem_limit_bytes=vmem)


def _resident(shape, imap):
    return pl.BlockSpec(shape, imap, pipeline_mode=pl.Buffered(1))


def _interleave(x):
    b, s, d = x.shape
    return x.reshape(b, NSEG, s // NSEG, d).transpose(0, 2, 1, 3).reshape(b, s, d)


def _deinterleave(x):
    b, s, d = x.shape
    return x.reshape(b, s // NSEG, NSEG, d).transpose(0, 2, 1, 3).reshape(b, s, d)


SCAN_UNROLL = 8


def _sigmoid(x):
    return 0.5 * jnp.tanh(0.5 * x) + 0.5


def _blk(i, n):
    return pl.ds(pl.multiple_of(i * n, n), n)


def _rms(xf, g):
    return xf * lax.rsqrt(jnp.mean(xf * xf, axis=-1, keepdims=True) + EPS) * g


def _inproj_body(x_ref, g_ref, w_ref, o_ref, xn_ref):
    @pl.when(pl.program_id(2) == 0)
    def _():
        xn_ref[...] = _rms(x_ref[...], g_ref[...]).astype(BF16)

    o_ref[...] = jnp.dot(xn_ref[...], w_ref[...], preferred_element_type=F32).astype(o_ref.dtype)


def _inproj(x, g, w, layer, tm, tn):
    b, s, d = x.shape
    n = w.shape[-1]
    tm = min(tm, s)
    return pl.pallas_call(
        _inproj_body,
        grid=(b, s // tm, n // tn),
        in_specs=[
            pl.BlockSpec((None, tm, d), lambda bi, i, j: (bi, i, 0)),
            pl.BlockSpec((None, 1, d), lambda bi, i, j: (layer, 0, 0)),
            pl.BlockSpec((None, d, tn), lambda bi, i, j: (layer, 0, j)),
        ],
        out_specs=pl.BlockSpec((None, tm, tn), lambda bi, i, j: (bi, i, j)),
        out_shape=jax.ShapeDtypeStruct((b, s, n), BF16),
        scratch_shapes=[pltpu.VMEM((tm, d), BF16)],
        compiler_params=_cparams(("parallel", "parallel", "arbitrary")),
        name="inproj",
    )(x, g, w)


def _s5_body(u_ref, m_ref, win_ref, wout_ref, av_ref, o_ref,
             ucat_ref, s_ref, h_ref, e_ref, g_ref, *, nq):
    nc = nq * NSEG
    ns = 4 * LANES
    for t in range(TCH):
        ucat_ref[:, t * LANES:(t + 1) * LANES] = u_ref[:, t].reshape(nc, LANES)
    ucat = ucat_ref[...]
    s_ref[...] = jnp.dot(ucat, win_ref[...], preferred_element_type=F32)

    a_fr, a_fi = av_ref[0:1, :], av_ref[1:2, :]
    a_br, a_bi = av_ref[2:3, :], av_ref[3:4, :]

    def scan(init, store):
        def step(q, carry):
            hfr, hfi, hbr, hbi = carry
            rf = q * NSEG
            rb = (nq - 1 - q) * NSEG
            if store:
                h_ref[pl.ds(rf, NSEG), 0 * ns:1 * ns] = hfr.astype(BF16)
                h_ref[pl.ds(rf, NSEG), 1 * ns:2 * ns] = hfi.astype(BF16)
                h_ref[pl.ds(rb, NSEG), 2 * ns:3 * ns] = hbr.astype(BF16)
                h_ref[pl.ds(rb, NSEG), 3 * ns:4 * ns] = hbi.astype(BF16)
            sfr = s_ref[pl.ds(rf, NSEG), 0 * ns:1 * ns]
            sfi = s_ref[pl.ds(rf, NSEG), 1 * ns:2 * ns]
            sbr = s_ref[pl.ds(rb, NSEG), 2 * ns:3 * ns]
            sbi = s_ref[pl.ds(rb, NSEG), 3 * ns:4 * ns]
            return (a_fr * hfr - a_fi * hfi + sfr, a_fr * hfi + a_fi * hfr + sfi,
                    a_br * hbr - a_bi * hbi + sbr, a_br * hbi + a_bi * hbr + sbi)
        carry = init
        for q in range(nq):
            carry = step(q, carry)
        return carry

    zero = jnp.zeros((NSEG, ns), F32)
    efr, efi, ebr, ebi = scan((zero, zero, zero, zero), False)
    e_ref[:, 0 * ns:1 * ns] = efr
    e_ref[:, 1 * ns:2 * ns] = efi
    e_ref[:, 2 * ns:3 * ns] = ebr
    e_ref[:, 3 * ns:4 * ns] = ebi

    p_fr, p_fi = av_ref[4:5, :], av_ref[5:6, :]
    p_br, p_bi = av_ref[6:7, :], av_ref[7:8, :]
    zrow = jnp.zeros((1, ns), F32)
    g_ref[0:1, 0:2 * ns] = jnp.zeros((1, 2 * ns), F32)
    g_ref[NSEG - 1:NSEG, 2 * ns:4 * ns] = jnp.zeros((1, 2 * ns), F32)
    gr, gi = zrow, zrow
    for j in range(NSEG - 1):
        er, ei = e_ref[j:j + 1, 0 * ns:1 * ns], e_ref[j:j + 1, 1 * ns:2 * ns]
        gr, gi = p_fr * gr - p_fi * gi + er, p_fr * gi + p_fi * gr + ei
        g_ref[j + 1:j + 2, 0 * ns:1 * ns] = gr
        g_ref[j + 1:j + 2, 1 * ns:2 * ns] = gi
    gr, gi = zrow, zrow
    for j in range(NSEG - 1, 0, -1):
        er, ei = e_ref[j:j + 1, 2 * ns:3 * ns], e_ref[j:j + 1, 3 * ns:4 * ns]
        gr, gi = p_br * gr - p_bi * gi + er, p_br * gi + p_bi * gr + ei
        g_ref[j - 1:j, 2 * ns:3 * ns] = gr
        g_ref[j - 1:j, 3 * ns:4 * ns] = gi

    scan((g_ref[:, 0 * ns:1 * ns], g_ref[:, 1 * ns:2 * ns],
          g_ref[:, 2 * ns:3 * ns], g_ref[:, 3 * ns:4 * ns]), True)

    y = jnp.dot(ucat, m_ref[...], preferred_element_type=F32)
    y = y + lax.dot_general(h_ref[...], wout_ref[...], (((1,), (1,)), ((), ())),
                            preferred_element_type=F32)
    for t in range(TCH):
        o_ref[:, t] = y[:, t * LANES:(t + 1) * LANES].reshape(nq, NSEG, LANES).astype(o_ref.dtype)


def _s5_mix(proj, col0, m, win, wout, av, layer, width):
    b, s, n = proj.shape
    ncb = width // LANES
    nq = s // (NSEG * TCH)
    nc = nq * NSEG
    kt = TCH * LANES
    nst = 16 * LANES
    proj5 = proj.reshape(b, nq, TCH, NSEG, n)
    out = pl.pallas_call(
        functools.partial(_s5_body, nq=nq),
        grid=(ncb, b),
        in_specs=[
            pl.BlockSpec((None, nq, TCH, NSEG, LANES), lambda c, bi: (bi, 0, 0, 0, col0 + c)),
            _resident((None, None, kt, kt), lambda c, bi: (layer, c, 0, 0)),
            _resident((None, None, kt, nst), lambda c, bi: (layer, c, 0, 0)),
            _resident((None, None, kt, nst), lambda c, bi: (layer, c, 0, 0)),
            pl.BlockSpec((None, None, 8, 4 * LANES), lambda c, bi: (layer, c, 0, 0)),
        ],
        out_specs=pl.BlockSpec((None, nq, TCH, NSEG, LANES), lambda c, bi: (bi, 0, 0, 0, c)),
        out_shape=jax.ShapeDtypeStruct((b, nq, TCH, NSEG, width), BF16),
        scratch_shapes=[
            pltpu.VMEM((nc, kt), BF16),
            pltpu.VMEM((nc, nst), F32),
            pltpu.VMEM((nc, nst), BF16),
            pltpu.VMEM((NSEG, nst), F32),
            pltpu.VMEM((NSEG, nst), F32),
        ],
        compiler_params=_cparams(("parallel", "parallel")),
        name="s5_mix",
    )(proj5, m, win, wout, av)
    return out.reshape(b, s, width)


def _rglru_body(xr_ref, cw_ref, par_ref, wg_ref, o_ref,
                pad_ref, tmp_ref, af_ref, bf_ref, ab_ref, bb_ref, *, s, tt):
    r = NSEG
    ls = s // r
    pad_ref[2 * r:2 * r + s, :] = xr_ref[...].astype(F32)
    tmp_ref[...] = jnp.zeros((3 * r, LANES), F32)
    tmp_ref[r:2 * r, :] = xr_ref[s - 2 * r:s - r, :].astype(F32)
    pad_ref[0:r, :] = tmp_ref[r - 1:2 * r - 1, :]
    tmp_ref[r:2 * r, :] = xr_ref[s - r:s, :].astype(F32)
    pad_ref[r:2 * r, :] = tmp_ref[r - 1:2 * r - 1, :]
    tmp_ref[r:2 * r, :] = xr_ref[0:r, :].astype(F32)
    pad_ref[2 * r + s:3 * r + s, :] = tmp_ref[r + 1:2 * r + 1, :]

    cb = par_ref[0:1, :]
    hb_af, hb_xf, hb_ab, hb_xb = (0.5 * par_ref[k:k + 1, :] for k in range(1, 5))
    c2_f = (-0.5 * LRU_C * LOG2_E) * jax.nn.softplus(-par_ref[5:6, :])
    c2_b = (-0.5 * LRU_C * LOG2_E) * jax.nn.softplus(-par_ref[6:7, :])

    def gates(k, _):
        r0 = pl.multiple_of(k * tt, tt)
        xc = cb + cw_ref[0:1, :] * pad_ref[pl.ds(r0, tt), :]
        for tap in range(1, 4):
            xc = xc + cw_ref[tap:tap + 1, :] * pad_ref[pl.ds(r0 + tap * r, tt), :]
        z = jnp.dot(xc.astype(BF16), wg_ref[...], preferred_element_type=F32)
        hx = 0.5 * xc
        for (a_ref, b_ref, c2, hba, hbx, off) in ((af_ref, bf_ref, c2_f, hb_af, hb_xf, 0),
                                                  (ab_ref, bb_ref, c2_b, hb_ab, hb_xb, 2)):
            tr = jnp.tanh(z[:, off * LANES:(off + 1) * LANES] + hba)
            tg = jnp.tanh(z[:, (off + 1) * LANES:(off + 2) * LANES] + hbx)
            a = jnp.exp2(c2 * tr + c2)
            v = 1.0 - a * a
            root = jnp.where(v > 0.0, v * lax.rsqrt(v), 0.0)
            a_ref[pl.ds(r0, tt), :] = a
            b_ref[pl.ds(r0, tt), :] = root * (hx * tg + hx)
        return 0

    lax.fori_loop(0, s // tt, gates, 0)

    def rows(i):
        return _blk(i, r)

    def local(i, carry):
        hf, pf, hb, pb = carry
        ib = ls - 1 - i
        a_f, a_b = af_ref[rows(i), :], ab_ref[rows(ib), :]
        return (a_f * hf + bf_ref[rows(i), :], a_f * pf, a_b * hb + bb_ref[rows(ib), :], a_b * pb)

    zero = jnp.zeros((r, LANES), F32)
    one = jnp.ones((r, LANES), F32)
    ef, pf, eb, pb = lax.fori_loop(0, ls, local, (zero, one, zero, one), unroll=SCAN_UNROLL)

    tmp_ref[0:r, :] = ef
    tmp_ref[r:2 * r, :] = pf
    g = jnp.zeros((1, LANES), F32)
    pad_ref[0:1, :] = g
    for j in range(r - 1):
        g = tmp_ref[r + j:r + j + 1, :] * g + tmp_ref[j:j + 1, :]
        pad_ref[j + 1:j + 2, :] = g
    gf = pad_ref[0:r, :]
    tmp_ref[0:r, :] = eb
    tmp_ref[r:2 * r, :] = pb
    g = jnp.zeros((1, LANES), F32)
    pad_ref[r - 1:r, :] = g
    for j in range(r - 1, 0, -1):
        g = tmp_ref[r + j:r + j + 1, :] * g + tmp_ref[j:j + 1, :]
        pad_ref[j - 1:j, :] = g
    gb = pad_ref[0:r, :]

    def both(i, carry):
        hf, hb = carry
        ib = ls - 1 - i
        hf = af_ref[rows(i), :] * hf + bf_ref[rows(i), :]
        bf_ref[rows(i), :] = hf
        hb = ab_ref[rows(ib), :] * hb + bb_ref[rows(ib), :]
        bb_ref[rows(ib), :] = hb
        return hf, hb

    lax.fori_loop(0, ls, both, (gf, gb), unroll=SCAN_UNROLL)

    def combine(k, _):
        rk = _blk(k, tt)
        o_ref[rk, :] = (bf_ref[rk, :] + bb_ref[rk, :]).astype(o_ref.dtype)
        return 0

    lax.fori_loop(0, s // tt, combine, 0)


def _rglru(proj, col_x, conv_w, par, wg, layer, width):
    b, s, n = proj.shape
    ncb = width // LANES
    tt = min(512, s)
    return pl.pallas_call(
        functools.partial(_rglru_body, s=s, tt=tt),
        grid=(ncb, b),
        in_specs=[
            pl.BlockSpec((None, s, LANES), lambda c, bi: (bi, 0, col_x + c)),
            pl.BlockSpec((None, 4, LANES), lambda c, bi: (layer, 0, c)),
            pl.BlockSpec((None, None, 8, LANES), lambda c, bi: (layer, c, 0, 0)),
            pl.BlockSpec((None, None, LANES, 4 * LANES), lambda c, bi: (layer, c, 0, 0)),
        ],
        out_specs=pl.BlockSpec((None, s, LANES), lambda c, bi: (bi, 0, c)),
        out_shape=jax.ShapeDtypeStruct((b, s, width), BF16),
        scratch_shapes=[
            pltpu.VMEM((s + 3 * NSEG, LANES), F32),
            pltpu.VMEM((3 * NSEG, LANES), F32),
            pltpu.VMEM((s, LANES), F32),
            pltpu.VMEM((s, LANES), F32),
            pltpu.VMEM((s, LANES), F32),
            pltpu.VMEM((s, LANES), F32),
        ],
        compiler_params=_cparams(("parallel", "parallel")),
        name="rglru",
    )(proj, conv_w, par, wg)


def _postmix_body(x_ref, ys_ref, u_ref, hb_ref, gr_ref, ga0_ref, ga1_ref, gb0_ref, gb1_ref, d_ref,
                  wglu_ref, wa_ref, wb_ref, wo_ref, o_ref):
    y = jax.nn.gelu(ys_ref[...].astype(F32) + d_ref[...] * u_ref[...].astype(F32))
    z = jnp.dot(y.astype(BF16), wglu_ref[...], preferred_element_type=F32)
    ya = (y * _sigmoid(z)).astype(BF16)
    pa = jnp.dot(ya, wa_ref[...], preferred_element_type=F32)
    yb = (hb_ref[...].astype(F32) * jax.nn.gelu(gr_ref[...].astype(F32))).astype(BF16)
    pb = jnp.dot(yb, wb_ref[...], preferred_element_type=F32)
    half = ga0_ref.shape[-1]
    merged = [(_sigmoid(ga_ref[...].astype(F32)) * pa[:, k * half:(k + 1) * half]
               + _sigmoid(gb_ref[...].astype(F32)) * pb[:, k * half:(k + 1) * half]).astype(BF16)
              for k, (ga_ref, gb_ref) in enumerate(((ga0_ref, gb0_ref), (ga1_ref, gb1_ref)))]
    o_ref[...] = x_ref[...] + jnp.dot(jnp.concatenate(merged, axis=1), wo_ref[...],
                                      preferred_element_type=F32)


def _postmix(x, ys, proj, yb, d, wglu, wa, wb, wo, layer, tm):
    b, s, dm = x.shape
    w5 = ys.shape[-1]
    wl = yb.shape[-1]
    tm = min(tm, s)
    half = dm // 2
    assert w5 == half and wl == half and proj.shape[-1] == 7 * half
    row = lambda bi, i: (bi, i, 0)
    gate = lambda k: pl.BlockSpec((None, tm, half), lambda bi, i: (bi, i, k))
    return pl.pallas_call(
        _postmix_body,
        grid=(b, s // tm),
        in_specs=[
            pl.BlockSpec((None, tm, dm), row),
            pl.BlockSpec((None, tm, w5), row),
            pl.BlockSpec((None, tm, w5), row),
            pl.BlockSpec((None, tm, wl), row),
            gate(2), gate(3), gate(4), gate(5), gate(6),
            pl.BlockSpec((None, 1, w5), lambda bi, i: (layer, 0, 0)),
            _resident((None, w5, w5), lambda bi, i: (layer, 0, 0)),
            _resident((None, w5, dm), lambda bi, i: (layer, 0, 0)),
            _resident((None, wl, dm), lambda bi, i: (layer, 0, 0)),
            _resident((None, dm, dm), lambda bi, i: (layer, 0, 0)),
        ],
        out_specs=pl.BlockSpec((None, tm, dm), row),
        out_shape=jax.ShapeDtypeStruct((b, s, dm), F32),
        compiler_params=_cparams(("parallel", "parallel")),
        name="postmix",
    )(x, ys, proj, yb, proj, proj, proj, proj, proj, d, wglu, wa, wb, wo)


def _ffn_body(x_ref, xp_ref, xq_ref, g_ref, wug_ref, wuv_ref, cwg_ref, cwv_ref, wd_ref, o_ref,
              xn_ref, edge_ref, *, tm):
    r = NSEG
    i = pl.program_id(1)

    @pl.when(pl.program_id(2) == 0)
    def _():
        g = g_ref[...]
        xn_ref[r:r + tm, :] = _rms(x_ref[...], g).astype(BF16)
        edge_ref[...] = jnp.zeros(edge_ref.shape, F32)
        prev = _rms(xp_ref[...], g)
        edge_ref[r:2 * r, :] = prev
        prev = jnp.where(i == 0, edge_ref[r - 1:2 * r - 1, :], prev)
        xn_ref[0:r, :] = prev.astype(BF16)
        nxt = _rms(xq_ref[...], g)
        edge_ref[r:2 * r, :] = nxt
        nxt = jnp.where(i == pl.num_programs(1) - 1, edge_ref[r + 1:2 * r + 1, :], nxt)
        xn_ref[r + tm:2 * r + tm, :] = nxt.astype(BF16)
        o_ref[...] = x_ref[...]

    xn = xn_ref[...]

    def conv(w_ref, cw_ref):
        h = jnp.dot(xn, w_ref[...], preferred_element_type=F32)
        return (cw_ref[0:1, :] * h[0:tm] + cw_ref[1:2, :] * h[r:r + tm]
                + cw_ref[2:3, :] * h[2 * r:2 * r + tm])

    act = (jax.nn.gelu(conv(wug_ref, cwg_ref)) * conv(wuv_ref, cwv_ref)).astype(BF16)
    o_ref[...] += jnp.dot(act, wd_ref[...], preferred_element_type=F32)


def _ffn(x, g, wup, cw, wdown, layer, tm, tf):
    b, s, dm = x.shape
    fh = wdown.shape[1]
    tm = min(tm, s)
    tf = min(tf, fh)
    nf = fh // tf
    gpt = tm // NSEG
    ng = s // NSEG
    return pl.pallas_call(
        functools.partial(_ffn_body, tm=tm),
        grid=(b, s // tm, nf),
        in_specs=[
            _resident((None, tm, dm), lambda bi, i, f: (bi, i, 0)),
            pl.BlockSpec((None, NSEG, dm), lambda bi, i, f: (bi, lax.rem(i * gpt + ng - 1, ng), 0)),
            pl.BlockSpec((None, NSEG, dm), lambda bi, i, f: (bi, lax.rem((i + 1) * gpt, ng), 0)),
            pl.BlockSpec((None, 1, dm), lambda bi, i, f: (layer, 0, 0)),
            pl.BlockSpec((None, dm, tf), lambda bi, i, f: (layer, 0, f)),
            pl.BlockSpec((None, dm, tf), lambda bi, i, f: (layer, 0, nf + f)),
            pl.BlockSpec((None, 3, tf), lambda bi, i, f: (layer, 0, f)),
            pl.BlockSpec((None, 3, tf), lambda bi, i, f: (layer, 0, nf + f)),
            pl.BlockSpec((None, tf, dm), lambda bi, i, f: (layer, f, 0)),
        ],
        out_specs=pl.BlockSpec((None, tm, dm), lambda bi, i, f: (bi, i, 0)),
        out_shape=jax.ShapeDtypeStruct((b, s, dm), F32),
        scratch_shapes=[pltpu.VMEM((tm + 2 * NSEG, dm), BF16), pltpu.VMEM((3 * NSEG, dm), F32)],
        compiler_params=_cparams(("parallel", "parallel", "arbitrary")),
        name="ffn",
    )(x, x, x, g, wup, wup, cw, cw, wdown)


def _final_body(x_ref, g_ref, o_ref):
    o_ref[...] = _rms(x_ref[...], g_ref[...])


def _final_norm(x, g, tm):
    b, s, dm = x.shape
    tm = min(tm, s)
    return pl.pallas_call(
        _final_body,
        grid=(b, s // tm),
        in_specs=[pl.BlockSpec((None, tm, dm), lambda bi, i: (bi, i, 0)),
                  pl.BlockSpec((1, dm), lambda bi, i: (0, 0))],
        out_specs=pl.BlockSpec((None, tm, dm), lambda bi, i: (bi, i, 0)),
        out_shape=jax.ShapeDtypeStruct((b, s, dm), F32),
        compiler_params=_cparams(("parallel", "parallel")),
        name="final_norm",
    )(x, g)


def _block_diag_embed(w, nblk, spec_in, spec_out):
    return jnp.einsum(spec_in + ",gh->" + spec_out, w, jnp.eye(nblk, dtype=w.dtype))


def _s5_prep_body(tab_ref, bb_ref, cc_ref, m_ref, win_ref, woutt_ref, p_ref, q_ref, d_ref, *, npz):
    t = TCH
    ns = tab_ref.shape[-1]
    lane = lax.broadcasted_iota(jnp.int32, (S5_H, 2 * ns), 1)
    grp = (lane % ns) // npz
    nt = (((1,), (1,)), ((), ()))

    def spread(dst, compact):
        for g in range(GROUPS_PER_BLOCK):
            dst[g * S5_H:(g + 1) * S5_H, :] = jnp.where(grp == g, compact, 0.0)

    for d in range(2):
        br, bi = bb_ref[d, 0], bb_ref[d, 1]
        cr, ci = cc_ref[d, 0], cc_ref[d, 1]
        for k in range(t + 1):
            ar, ai = tab_ref[d, 0, k:k + 1, :], tab_ref[d, 1, k:k + 1, :]
            if k < t:
                spread(p_ref.at[d, k], jnp.concatenate([ar * br - ai * bi, ar * bi + ai * br], axis=1))
            spread(q_ref.at[d, k], jnp.concatenate([cr * ar - ci * ai, -(cr * ai + ci * ar)], axis=1))
        for k in range(t):
            pk, q0 = p_ref[d, k], q_ref[d, 0]
            ph, qh = pk.astype(BF16), q0.astype(BF16)
            pl_, ql = (pk - ph.astype(F32)).astype(BF16), (q0 - qh.astype(F32)).astype(BF16)
            d_ref[d, k] = (lax.dot_general(ph, qh, nt, preferred_element_type=F32)
                           + lax.dot_general(ph, ql, nt, preferred_element_type=F32)
                           + lax.dot_general(pl_, qh, nt, preferred_element_type=F32))

    for s in range(t):
        rows = slice(s * LANES, (s + 1) * LANES)
        for to in range(t):
            if to > s:
                blk = d_ref[0, to - s]
            elif to < s:
                blk = d_ref[1, s - to]
            else:
                blk = d_ref[0, 0] + d_ref[1, 0]
            m_ref[rows, to * LANES:(to + 1) * LANES] = blk.astype(BF16)
        win_ref[rows, 0:2 * ns] = p_ref[0, t - 1 - s].astype(BF16)
        win_ref[rows, 2 * ns:4 * ns] = p_ref[1, s].astype(BF16)
        woutt_ref[rows, 0:2 * ns] = q_ref[0, s + 1].astype(BF16)
        woutt_ref[rows, 2 * ns:4 * ns] = q_ref[1, t - s].astype(BF16)


def _prep_s5(a_re, a_im, log_dt, b_re, b_im, c_re, c_im, seg_lens):
    nl, _, ng, npz = a_re.shape
    gb = GROUPS_PER_BLOCK
    ncb = ng // gb
    ns = gb * npz
    kt = TCH * LANES
    lr, li = a_re.astype(F32), a_im.astype(F32)
    dt = jnp.exp(log_dt.astype(F32))[..., None]
    zr, zi = lr * dt, li * dt

    def cpow(k, re=zr, im=zi):
        mag = jnp.exp(re * k)
        return mag * jnp.cos(im * k), mag * jnp.sin(im * k)

    kv = jnp.arange(S5_H, dtype=F32)[:, None]
    tab = jnp.stack(cpow(kv, zr[..., None, :], zi[..., None, :]), axis=2)
    ar, ai = cpow(1.0)
    den = lr * lr + li * li
    fr = ((ar - 1.0) * lr + ai * li) / den
    fi = (ai * lr - (ar - 1.0) * li) / den
    b_re, b_im = b_re.astype(F32), b_im.astype(F32)
    bbr = fr[..., None] * b_re - fi[..., None] * b_im
    bbi = fr[..., None] * b_im + fi[..., None] * b_re
    bb = jnp.swapaxes(jnp.stack([bbr, bbi], axis=2), -1, -2)
    cc = jnp.stack([c_re.astype(F32), c_im.astype(F32)], axis=2)

    def lanes(v):
        v = v.reshape(nl, 2, 2, ncb, gb, S5_H, npz).transpose(0, 3, 1, 2, 5, 4, 6)
        return v.reshape(nl, ncb, 2, 2, S5_H, ns)

    tab_spec = pl.BlockSpec((None, None, 2, 2, S5_H, ns), lambda l, c: (l, c, 0, 0, 0, 0))
    w_spec = pl.BlockSpec((None, None, kt, 4 * ns), lambda l, c: (l, c, 0, 0))
    m, win, woutt = pl.pallas_call(
        functools.partial(_s5_prep_body, npz=npz),
        grid=(nl, ncb),
        in_specs=[tab_spec, tab_spec, tab_spec],
        out_specs=[pl.BlockSpec((None, None, kt, kt), lambda l, c: (l, c, 0, 0)), w_spec, w_spec],
        out_shape=[jax.ShapeDtypeStruct((nl, ncb, kt, kt), BF16),
                   jax.ShapeDtypeStruct((nl, ncb, kt, 4 * ns), BF16),
                   jax.ShapeDtypeStruct((nl, ncb, kt, 4 * ns), BF16)],
        scratch_shapes=[pltpu.VMEM((2, TCH, LANES, 2 * ns), F32),
                        pltpu.VMEM((2, TCH + 1, LANES, 2 * ns), F32),
                        pltpu.VMEM((2, TCH, LANES, LANES), F32)],
        compiler_params=_cparams(("parallel", "parallel")),
        name="s5_prep",
    )(lanes(tab), lanes(bb), lanes(cc))

    def vec(vr, vi):
        v = jnp.stack([vr[:, 0], vi[:, 0], vr[:, 1], vi[:, 1]], axis=1)
        return v.reshape(nl, 4, ncb, ns).transpose(0, 2, 1, 3)

    avs = [jnp.concatenate([vec(*cpow(float(TCH))), vec(*cpow(float(ls)))], axis=2) for ls in seg_lens]
    return m, win, woutt, avs


def _prep_lru(conv_b, w_a, b_a, w_x, b_x, lam):
    nl, _, nb, blk, _ = w_a.shape
    width = nb * blk
    ncb = width // LANES
    per = LANES // blk

    def bd(w):
        w = w.reshape(nl, ncb, per, blk, blk)
        return _block_diag_embed(w, per, "lcgij", "lcgihj").reshape(nl, ncb, LANES, LANES)

    wg = jnp.concatenate([bd(w_a[:, 0]), bd(w_x[:, 0]), bd(w_a[:, 1]), bd(w_x[:, 1])], axis=-1)
    rows = [conv_b, b_a[:, 0], b_x[:, 0], b_a[:, 1], b_x[:, 1], lam[:, 0], lam[:, 1],
            jnp.zeros_like(conv_b)]
    par = jnp.stack(rows, axis=1).astype(F32)
    par = par.reshape(nl, 8, ncb, LANES).transpose(0, 2, 1, 3)
    return (0.5 * wg).astype(BF16), par


def kernel(x_prompt, x_sample, norm1_g, w_in, s5_a_re, s5_a_im, s5_log_dt, s5_b_re, s5_b_im, s5_c_re, s5_c_im, s5_d, s5_w_glu, lru_conv_w, lru_conv_b, lru_w_a, lru_b_a, lru_w_x, lru_b_x, lru_lambda, w_proj_a, w_proj_b, w_out, norm2_g, ffn_w_up, ffn_conv_w, ffn_w_down, final_g):
    depth, dm, _ = w_in.shape
    w5 = s5_d.shape[-1]
    wl = lru_conv_b.shape[-1]
    groups = (x_prompt, x_sample)
    seg_lens = [x.shape[1] // NSEG for x in groups]

    w_in_p = w_in.astype(BF16)
    col_u = 0
    col_x = w5 // LANES
    col_g = col_x + wl // LANES
    m, win, wout, avs = _prep_s5(s5_a_re, s5_a_im, s5_log_dt, s5_b_re, s5_b_im, s5_c_re, s5_c_im, seg_lens)
    wg, par = _prep_lru(lru_conv_b, lru_w_a, lru_b_a, lru_w_x, lru_b_x, lru_lambda)
    conv_w = lru_conv_w.astype(F32)
    wglu, wa, wb, wo = (w.astype(BF16) for w in (s5_w_glu, w_proj_a, w_proj_b, w_out))
    wup, wdown = ffn_w_up.astype(BF16), ffn_w_down.astype(BF16)
    ffn_cw = ffn_conv_w.astype(F32)
    g1 = norm1_g.astype(F32)[:, None, :]
    g2 = norm2_g.astype(F32)[:, None, :]
    d5 = s5_d.astype(F32)[:, None, :]
    gfin = final_g.astype(F32)[None, :]

    outs = []
    for x, av in zip(groups, avs):
        x = _interleave(x.astype(F32))
        for l in range(depth):
            proj = _inproj(x, g1, w_in_p, l, tm=1024, tn=min(1024, w5))
            ys = _s5_mix(proj, col_u, m, win, wout, av, l, w5)
            hb = _rglru(proj, col_x, conv_w, par, wg, l, wl)
            x = _postmix(x, ys, proj, hb, d5, wglu, wa, wb, wo, l, tm=256)
            x = _ffn(x, g2, wup, ffn_cw, wdown, l, tm=1024, tf=512)
        outs.append(_deinterleave(_final_norm(x, gfin, tm=512)))
    return tuple(outs)
```

```python
import functools
import math

import jax
import jax.numpy as jnp
from jax import lax
from jax.experimental import pallas as pl
from jax.experimental.pallas import tpu as pltpu

F32 = jnp.float32
BF16 = jnp.bfloat16

LANES = 128
MXU_COLS = 256
NSEG = 16
TCH = 8
S5_H = 16
GROUPS_PER_BLOCK = LANES // S5_H
EPS = 1e-6
LRU_C = 8.0
LOG2_E = 1.0 / math.log(2.0)
VMEM_BYTES = 64 * 1024 * 1024
VMEM_LIMIT = 60 * 1024 * 1024


def _cparams(sem, vmem=VMEM_LIMIT):
    return pltpu.CompilerParams(dimension_semantics=sem, vmem_limit_bytes=vmem)


def _resident(shape, imap):
    return pl.BlockSpec(shape, imap, pipeline_mode=pl.Buffered(1))


def _interleave(x):
    b, s, d = x.shape
    return x.reshape(b, NSEG, s // NSEG, d).transpose(0, 2, 1, 3).reshape(b, s, d)


def _deinterleave(x):
    b, s, d = x.shape
    return x.reshape(b, s // NSEG, NSEG, d).transpose(0, 2, 1, 3).reshape(b, s, d)


SCAN_UNROLL = 8


def _sigmoid(x):
    return 0.5 * jnp.tanh(0.5 * x) + 0.5


def _blk(i, n):
    return pl.ds(i * n, n) if isinstance(i, int) else pl.ds(pl.multiple_of(i * n, n), n)


def _rms(xf, g):
    return xf * lax.rsqrt(jnp.mean(xf * xf, axis=-1, keepdims=True) + EPS) * g


def _inproj_body(x_ref, g_ref, w_ref, o_ref, xn_ref):
    @pl.when(pl.program_id(2) == 0)
    def _():
        xn_ref[...] = _rms(x_ref[...], g_ref[...]).astype(BF16)

    o_ref[...] = jnp.dot(xn_ref[...], w_ref[...], preferred_element_type=F32).astype(o_ref.dtype)


def _inproj(x, g, w, layer, tm, tn):
    b, s, d = x.shape
    n = w.shape[-1]
    tm = min(tm, s)
    return pl.pallas_call(
        _inproj_body,
        grid=(b, s // tm, n // tn),
        in_specs=[
            pl.BlockSpec((None, tm, d), lambda bi, i, j: (bi, i, 0)),
            pl.BlockSpec((None, 1, d), lambda bi, i, j: (layer, 0, 0)),
            pl.BlockSpec((None, d, tn), lambda bi, i, j: (layer, 0, j)),
        ],
        out_specs=pl.BlockSpec((None, tm, tn), lambda bi, i, j: (bi, i, j)),
        out_shape=jax.ShapeDtypeStruct((b, s, n), BF16),
        scratch_shapes=[pltpu.VMEM((tm, d), BF16)],
        compiler_params=_cparams(("parallel", "parallel", "arbitrary")),
        name="inproj",
    )(x, g, w)


def _s5_body(u_ref, m_ref, win_ref, wout_ref, av_ref, o_ref,
             ucat_ref, s_ref, h_ref, e_ref, g_ref, *, nq, work=None):
    nc = nq * NSEG
    ns = 4 * LANES
    for t in range(TCH):
        ucat_ref[:, t * LANES:(t + 1) * LANES] = u_ref[:, t].reshape(nc, LANES)
    ucat = ucat_ref[...]
    work = list(work or ())
    nwc = max(min(len(work), win_ref.shape[-1] // MXU_COLS), 1)
    wc = win_ref.shape[-1] // nwc
    per_chunk = -(-len(work) // nwc)
    for k in range(nwc):
        cols = slice(k * wc, (k + 1) * wc)
        s_ref[:, cols] = jnp.dot(ucat, win_ref[:, cols], preferred_element_type=F32)
        for emit in work[k * per_chunk:(k + 1) * per_chunk]:
            emit()

    a_fr, a_fi = av_ref[0:1, :], av_ref[1:2, :]
    a_br, a_bi = av_ref[2:3, :], av_ref[3:4, :]

    def scan(init, store):
        def step(q, carry):
            hfr, hfi, hbr, hbi = carry
            rf = q * NSEG
            rb = (nq - 1 - q) * NSEG
            if store:
                h_ref[pl.ds(rf, NSEG), 0 * ns:1 * ns] = hfr.astype(BF16)
                h_ref[pl.ds(rf, NSEG), 1 * ns:2 * ns] = hfi.astype(BF16)
                h_ref[pl.ds(rb, NSEG), 2 * ns:3 * ns] = hbr.astype(BF16)
                h_ref[pl.ds(rb, NSEG), 3 * ns:4 * ns] = hbi.astype(BF16)
            sfr = s_ref[pl.ds(rf, NSEG), 0 * ns:1 * ns]
            sfi = s_ref[pl.ds(rf, NSEG), 1 * ns:2 * ns]
            sbr = s_ref[pl.ds(rb, NSEG), 2 * ns:3 * ns]
            sbi = s_ref[pl.ds(rb, NSEG), 3 * ns:4 * ns]
            return (a_fr * hfr - a_fi * hfi + sfr, a_fr * hfi + a_fi * hfr + sfi,
                    a_br * hbr - a_bi * hbi + sbr, a_br * hbi + a_bi * hbr + sbi)
        carry = init
        for q in range(nq):
            carry = step(q, carry)
        return carry

    zero = jnp.zeros((NSEG, ns), F32)
    efr, efi, ebr, ebi = scan((zero, zero, zero, zero), False)
    e_ref[:, 0 * ns:1 * ns] = efr
    e_ref[:, 1 * ns:2 * ns] = efi
    e_ref[:, 2 * ns:3 * ns] = ebr
    e_ref[:, 3 * ns:4 * ns] = ebi

    p_fr, p_fi = av_ref[4:5, :], av_ref[5:6, :]
    p_br, p_bi = av_ref[6:7, :], av_ref[7:8, :]
    zrow = jnp.zeros((1, ns), F32)
    g_ref[0:1, 0:2 * ns] = jnp.zeros((1, 2 * ns), F32)
    g_ref[NSEG - 1:NSEG, 2 * ns:4 * ns] = jnp.zeros((1, 2 * ns), F32)
    gr, gi = zrow, zrow
    for j in range(NSEG - 1):
        er, ei = e_ref[j:j + 1, 0 * ns:1 * ns], e_ref[j:j + 1, 1 * ns:2 * ns]
        gr, gi = p_fr * gr - p_fi * gi + er, p_fr * gi + p_fi * gr + ei
        g_ref[j + 1:j + 2, 0 * ns:1 * ns] = gr
        g_ref[j + 1:j + 2, 1 * ns:2 * ns] = gi
    gr, gi = zrow, zrow
    for j in range(NSEG - 1, 0, -1):
        er, ei = e_ref[j:j + 1, 2 * ns:3 * ns], e_ref[j:j + 1, 3 * ns:4 * ns]
        gr, gi = p_br * gr - p_bi * gi + er, p_br * gi + p_bi * gr + ei
        g_ref[j - 1:j, 2 * ns:3 * ns] = gr
        g_ref[j - 1:j, 3 * ns:4 * ns] = gi

    scan((g_ref[:, 0 * ns:1 * ns], g_ref[:, 1 * ns:2 * ns],
          g_ref[:, 2 * ns:3 * ns], g_ref[:, 3 * ns:4 * ns]), True)

    y = jnp.dot(ucat, m_ref[...], preferred_element_type=F32)
    y = y + lax.dot_general(h_ref[...], wout_ref[...], (((1,), (1,)), ((), ())),
                            preferred_element_type=F32)
    for t in range(TCH):
        o_ref[:, t] = y[:, t * LANES:(t + 1) * LANES].reshape(nq, NSEG, LANES).astype(o_ref.dtype)


def _s5_mix(proj, col0, m, win, wout, av, layer, width):
    b, s, n = proj.shape
    ncb = width // LANES
    nq = s // (NSEG * TCH)
    nc = nq * NSEG
    kt = TCH * LANES
    nst = 16 * LANES
    proj5 = proj.reshape(b, nq, TCH, NSEG, n)
    out = pl.pallas_call(
        functools.partial(_s5_body, nq=nq),
        grid=(ncb, b),
        in_specs=[
            pl.BlockSpec((None, nq, TCH, NSEG, LANES), lambda c, bi: (bi, 0, 0, 0, col0 + c)),
            _resident((None, None, kt, kt), lambda c, bi: (layer, c, 0, 0)),
            _resident((None, None, kt, nst), lambda c, bi: (layer, c, 0, 0)),
            _resident((None, None, kt, nst), lambda c, bi: (layer, c, 0, 0)),
            pl.BlockSpec((None, None, 8, 4 * LANES), lambda c, bi: (layer, c, 0, 0)),
        ],
        out_specs=pl.BlockSpec((None, nq, TCH, NSEG, LANES), lambda c, bi: (bi, 0, 0, 0, c)),
        out_shape=jax.ShapeDtypeStruct((b, nq, TCH, NSEG, width), BF16),
        scratch_shapes=[
            pltpu.VMEM((nc, kt), BF16),
            pltpu.VMEM((nc, nst), F32),
            pltpu.VMEM((nc, nst), BF16),
            pltpu.VMEM((NSEG, nst), F32),
            pltpu.VMEM((NSEG, nst), F32),
        ],
        compiler_params=_cparams(("parallel", "parallel")),
        name="s5_mix",
    )(proj5, m, win, wout, av)
    return out.reshape(b, s, width)


def _rglru_body(xr_ref, cw_ref, par_ref, wg_ref, o_ref,
                pad_ref, tmp_ref, af_ref, bf_ref, ab_ref, bb_ref, *, s, tt, schedule=None):
    r = NSEG
    ls = s // r
    pad_ref[2 * r:2 * r + s, :] = xr_ref[...].astype(F32)
    tmp_ref[...] = jnp.zeros((3 * r, LANES), F32)
    tmp_ref[r:2 * r, :] = xr_ref[s - 2 * r:s - r, :].astype(F32)
    pad_ref[0:r, :] = tmp_ref[r - 1:2 * r - 1, :]
    tmp_ref[r:2 * r, :] = xr_ref[s - r:s, :].astype(F32)
    pad_ref[r:2 * r, :] = tmp_ref[r - 1:2 * r - 1, :]
    tmp_ref[r:2 * r, :] = xr_ref[0:r, :].astype(F32)
    pad_ref[2 * r + s:3 * r + s, :] = tmp_ref[r + 1:2 * r + 1, :]

    cb = par_ref[0:1, :]
    hb_af, hb_xf, hb_ab, hb_xb = (0.5 * par_ref[k:k + 1, :] for k in range(1, 5))
    c2_f = (-0.5 * LRU_C * LOG2_E) * jax.nn.softplus(-par_ref[5:6, :])
    c2_b = (-0.5 * LRU_C * LOG2_E) * jax.nn.softplus(-par_ref[6:7, :])

    def gates(k, _):
        r0 = k * tt if isinstance(k, int) else pl.multiple_of(k * tt, tt)
        xc = cb + cw_ref[0:1, :] * pad_ref[pl.ds(r0, tt), :]
        for tap in range(1, 4):
            xc = xc + cw_ref[tap:tap + 1, :] * pad_ref[pl.ds(r0 + tap * r, tt), :]
        z = jnp.dot(xc.astype(BF16), wg_ref[...], preferred_element_type=F32)
        hx = 0.5 * xc
        for (a_ref, b_ref, c2, hba, hbx, off) in ((af_ref, bf_ref, c2_f, hb_af, hb_xf, 0),
                                                  (ab_ref, bb_ref, c2_b, hb_ab, hb_xb, 2)):
            tr = jnp.tanh(z[:, off * LANES:(off + 1) * LANES] + hba)
            tg = jnp.tanh(z[:, (off + 1) * LANES:(off + 2) * LANES] + hbx)
            a = jnp.exp2(c2 * tr + c2)
            v = 1.0 - a * a
            root = jnp.where(v > 0.0, v * lax.rsqrt(v), 0.0)
            a_ref[pl.ds(r0, tt), :] = a
            b_ref[pl.ds(r0, tt), :] = root * (hx * tg + hx)
        return 0

    def rows(i):
        return _blk(i, r)

    def local(i, carry):
        hf, pf, hb, pb = carry
        ib = ls - 1 - i
        a_f, a_b = af_ref[rows(i), :], ab_ref[rows(ib), :]
        return (a_f * hf + bf_ref[rows(i), :], a_f * pf, a_b * hb + bb_ref[rows(ib), :], a_b * pb)

    start = (jnp.zeros((r, LANES), F32), jnp.ones((r, LANES), F32)) * 2

    def carries(ef, pf, eb, pb):
        tmp_ref[0:r, :] = ef
        tmp_ref[r:2 * r, :] = pf
        g = jnp.zeros((1, LANES), F32)
        pad_ref[0:1, :] = g
        for j in range(r - 1):
            g = tmp_ref[r + j:r + j + 1, :] * g + tmp_ref[j:j + 1, :]
            pad_ref[j + 1:j + 2, :] = g
        gf = pad_ref[0:r, :]
        tmp_ref[0:r, :] = eb
        tmp_ref[r:2 * r, :] = pb
        g = jnp.zeros((1, LANES), F32)
        pad_ref[r - 1:r, :] = g
        for j in range(r - 1, 0, -1):
            g = tmp_ref[r + j:r + j + 1, :] * g + tmp_ref[j:j + 1, :]
            pad_ref[j - 1:j, :] = g
        return gf, pad_ref[0:r, :]

    def both(i, carry):
        hf, hb = carry
        ib = ls - 1 - i
        hf = af_ref[rows(i), :] * hf + bf_ref[rows(i), :]
        bf_ref[rows(i), :] = hf
        hb = ab_ref[rows(ib), :] * hb + bb_ref[rows(ib), :]
        bb_ref[rows(ib), :] = hb
        return hf, hb

    def combine(k, _):
        rk = _blk(k, tt)
        o_ref[rk, :] = (bf_ref[rk, :] + bb_ref[rk, :]).astype(o_ref.dtype)
        return 0

    nt = s // tt
    if schedule is None:
        lax.fori_loop(0, nt, gates, 0)
    else:
        schedule([functools.partial(gates, k, 0) for k in range(nt)])
    ends = lax.fori_loop(0, ls, local, start, unroll=SCAN_UNROLL)
    lax.fori_loop(0, ls, both, carries(*ends), unroll=SCAN_UNROLL)
    lax.fori_loop(0, nt, combine, 0)


def _rglru(proj, col_x, conv_w, par, wg, layer, width):
    b, s, n = proj.shape
    ncb = width // LANES
    tt = min(512, s)
    return pl.pallas_call(
        functools.partial(_rglru_body, s=s, tt=tt),
        grid=(ncb, b),
        in_specs=[
            pl.BlockSpec((None, s, LANES), lambda c, bi: (bi, 0, col_x + c)),
            pl.BlockSpec((None, 4, LANES), lambda c, bi: (layer, 0, c)),
            pl.BlockSpec((None, None, 8, LANES), lambda c, bi: (layer, c, 0, 0)),
            pl.BlockSpec((None, None, LANES, 4 * LANES), lambda c, bi: (layer, c, 0, 0)),
        ],
        out_specs=pl.BlockSpec((None, s, LANES), lambda c, bi: (bi, 0, c)),
        out_shape=jax.ShapeDtypeStruct((b, s, width), BF16),
        scratch_shapes=[
            pltpu.VMEM((s + 3 * NSEG, LANES), F32),
            pltpu.VMEM((3 * NSEG, LANES), F32),
            pltpu.VMEM((s, LANES), F32),
            pltpu.VMEM((s, LANES), F32),
            pltpu.VMEM((s, LANES), F32),
            pltpu.VMEM((s, LANES), F32),
        ],
        compiler_params=_cparams(("parallel", "parallel")),
        name="rglru",
    )(proj, conv_w, par, wg)


def _mixers_body(u_ref, xr_ref, m_ref, win_ref, wout_ref, av_ref, cw_ref, par_ref, wg_ref,
                 ys_ref, hb_ref, ucat_ref, s_ref, h_ref, e_ref, g_ref,
                 pad_ref, tmp_ref, af_ref, bf_ref, ab_ref, bb_ref, *, nq, s, tt):
    def s5_with(items):
        _s5_body(u_ref, m_ref, win_ref, wout_ref, av_ref, ys_ref, ucat_ref, s_ref, h_ref, e_ref, g_ref,
                 nq=nq, work=items)

    _rglru_body(xr_ref, cw_ref, par_ref, wg_ref, hb_ref,
                pad_ref, tmp_ref, af_ref, bf_ref, ab_ref, bb_ref, s=s, tt=tt, schedule=s5_with)


def _mixers_vmem_bytes(s):
    kt, nst = TCH * LANES, 16 * LANES
    weights = 2 * (kt * kt + 2 * kt * nst)
    s5_rows = s * (4 * 2 * LANES) + (s // TCH) * (2 * kt + 4 * nst + 2 * nst)
    lru_rows = s * LANES * (2 * 2 * 2 + 5 * 4)
    return weights + s5_rows + lru_rows


def _mixers(proj, col_u, col_x, m, win, wout, av, conv_w, par, wg, layer, width):
    b, s, n = proj.shape
    ncb = width // LANES
    nq = s // (NSEG * TCH)
    nc = nq * NSEG
    kt = TCH * LANES
    nst = 16 * LANES
    tt = min(512, s)
    proj5 = proj.reshape(b, nq, TCH, NSEG, n)
    ys, hb = pl.pallas_call(
        functools.partial(_mixers_body, nq=nq, s=s, tt=tt),
        grid=(ncb, b),
        in_specs=[
            pl.BlockSpec((None, nq, TCH, NSEG, LANES), lambda c, bi: (bi, 0, 0, 0, col_u + c)),
            pl.BlockSpec((None, s, LANES), lambda c, bi: (bi, 0, col_x + c)),
            _resident((None, None, kt, kt), lambda c, bi: (layer, c, 0, 0)),
            _resident((None, None, kt, nst), lambda c, bi: (layer, c, 0, 0)),
            _resident((None, None, kt, nst), lambda c, bi: (layer, c, 0, 0)),
            pl.BlockSpec((None, None, 8, 4 * LANES), lambda c, bi: (layer, c, 0, 0)),
            pl.BlockSpec((None, 4, LANES), lambda c, bi: (layer, 0, c)),
            pl.BlockSpec((None, None, 8, LANES), lambda c, bi: (layer, c, 0, 0)),
            pl.BlockSpec((None, None, LANES, 4 * LANES), lambda c, bi: (layer, c, 0, 0)),
        ],
        out_specs=[pl.BlockSpec((None, nq, TCH, NSEG, LANES), lambda c, bi: (bi, 0, 0, 0, c)),
                   pl.BlockSpec((None, s, LANES), lambda c, bi: (bi, 0, c))],
        out_shape=[jax.ShapeDtypeStruct((b, nq, TCH, NSEG, width), BF16),
                   jax.ShapeDtypeStruct((b, s, width), BF16)],
        scratch_shapes=[
            pltpu.VMEM((nc, kt), BF16),
            pltpu.VMEM((nc, nst), F32),
            pltpu.VMEM((nc, nst), BF16),
            pltpu.VMEM((NSEG, nst), F32),
            pltpu.VMEM((NSEG, nst), F32),
            pltpu.VMEM((s + 3 * NSEG, LANES), F32),
            pltpu.VMEM((3 * NSEG, LANES), F32),
            pltpu.VMEM((s, LANES), F32),
            pltpu.VMEM((s, LANES), F32),
            pltpu.VMEM((s, LANES), F32),
            pltpu.VMEM((s, LANES), F32),
        ],
        compiler_params=_cparams(("parallel", "parallel")),
        name="mixers",
    )(proj5, proj, m, win, wout, av, conv_w, par, wg)
    return ys.reshape(b, s, width), hb


def _postmix_body(x_ref, ys_ref, u_ref, hb_ref, gr_ref, ga0_ref, ga1_ref, gb0_ref, gb1_ref, d_ref,
                  wglu_ref, wa_ref, wb_ref, wo_ref, o_ref):
    y = jax.nn.gelu(ys_ref[...].astype(F32) + d_ref[...] * u_ref[...].astype(F32))
    z = jnp.dot(y.astype(BF16), wglu_ref[...], preferred_element_type=F32)
    ya = (y * _sigmoid(z)).astype(BF16)
    pa = jnp.dot(ya, wa_ref[...], preferred_element_type=F32)
    yb = (hb_ref[...].astype(F32) * jax.nn.gelu(gr_ref[...].astype(F32))).astype(BF16)
    pb = jnp.dot(yb, wb_ref[...], preferred_element_type=F32)
    half = ga0_ref.shape[-1]
    merged = [(_sigmoid(ga_ref[...].astype(F32)) * pa[:, k * half:(k + 1) * half]
               + _sigmoid(gb_ref[...].astype(F32)) * pb[:, k * half:(k + 1) * half]).astype(BF16)
              for k, (ga_ref, gb_ref) in enumerate(((ga0_ref, gb0_ref), (ga1_ref, gb1_ref)))]
    o_ref[...] = x_ref[...] + jnp.dot(jnp.concatenate(merged, axis=1), wo_ref[...],
                                      preferred_element_type=F32)


def _postmix(x, ys, proj, yb, d, wglu, wa, wb, wo, layer, tm):
    b, s, dm = x.shape
    w5 = ys.shape[-1]
    wl = yb.shape[-1]
    tm = min(tm, s)
    half = dm // 2
    assert w5 == half and wl == half and proj.shape[-1] == 7 * half
    row = lambda bi, i: (bi, i, 0)
    gate = lambda k: pl.BlockSpec((None, tm, half), lambda bi, i: (bi, i, k))
    return pl.pallas_call(
        _postmix_body,
        grid=(b, s // tm),
        in_specs=[
            pl.BlockSpec((None, tm, dm), row),
            pl.BlockSpec((None, tm, w5), row),
            pl.BlockSpec((None, tm, w5), row),
            pl.BlockSpec((None, tm, wl), row),
            gate(2), gate(3), gate(4), gate(5), gate(6),
            pl.BlockSpec((None, 1, w5), lambda bi, i: (layer, 0, 0)),
            _resident((None, w5, w5), lambda bi, i: (layer, 0, 0)),
            _resident((None, w5, dm), lambda bi, i: (layer, 0, 0)),
            _resident((None, wl, dm), lambda bi, i: (layer, 0, 0)),
            _resident((None, dm, dm), lambda bi, i: (layer, 0, 0)),
        ],
        out_specs=pl.BlockSpec((None, tm, dm), row),
        out_shape=jax.ShapeDtypeStruct((b, s, dm), F32),
        compiler_params=_cparams(("parallel", "parallel")),
        name="postmix",
    )(x, ys, proj, yb, proj, proj, proj, proj, proj, d, wglu, wa, wb, wo)


def _ffn_body(x_ref, xp_ref, xq_ref, g_ref, wug_ref, wuv_ref, cwg_ref, cwv_ref, wd_ref, o_ref,
              xn_ref, edge_ref, *, tm):
    r = NSEG
    i = pl.program_id(1)

    @pl.when(pl.program_id(2) == 0)
    def _():
        g = g_ref[...]
        xn_ref[r:r + tm, :] = _rms(x_ref[...], g).astype(BF16)
        edge_ref[...] = jnp.zeros(edge_ref.shape, F32)
        prev = _rms(xp_ref[...], g)
        edge_ref[r:2 * r, :] = prev
        prev = jnp.where(i == 0, edge_ref[r - 1:2 * r - 1, :], prev)
        xn_ref[0:r, :] = prev.astype(BF16)
        nxt = _rms(xq_ref[...], g)
        edge_ref[r:2 * r, :] = nxt
        nxt = jnp.where(i == pl.num_programs(1) - 1, edge_ref[r + 1:2 * r + 1, :], nxt)
        xn_ref[r + tm:2 * r + tm, :] = nxt.astype(BF16)
        o_ref[...] = x_ref[...]

    xn = xn_ref[...]

    def conv(w_ref, cw_ref):
        h = jnp.dot(xn, w_ref[...], preferred_element_type=F32)
        return (cw_ref[0:1, :] * h[0:tm] + cw_ref[1:2, :] * h[r:r + tm]
                + cw_ref[2:3, :] * h[2 * r:2 * r + tm])

    act = (jax.nn.gelu(conv(wug_ref, cwg_ref)) * conv(wuv_ref, cwv_ref)).astype(BF16)
    o_ref[...] += jnp.dot(act, wd_ref[...], preferred_element_type=F32)


def _ffn(x, g, wup, cw, wdown, layer, tm, tf):
    b, s, dm = x.shape
    fh = wdown.shape[1]
    tm = min(tm, s)
    tf = min(tf, fh)
    nf = fh // tf
    gpt = tm // NSEG
    ng = s // NSEG
    return pl.pallas_call(
        functools.partial(_ffn_body, tm=tm),
        grid=(b, s // tm, nf),
        in_specs=[
            _resident((None, tm, dm), lambda bi, i, f: (bi, i, 0)),
            pl.BlockSpec((None, NSEG, dm), lambda bi, i, f: (bi, lax.rem(i * gpt + ng - 1, ng), 0)),
            pl.BlockSpec((None, NSEG, dm), lambda bi, i, f: (bi, lax.rem((i + 1) * gpt, ng), 0)),
            pl.BlockSpec((None, 1, dm), lambda bi, i, f: (layer, 0, 0)),
            pl.BlockSpec((None, dm, tf), lambda bi, i, f: (layer, 0, f)),
            pl.BlockSpec((None, dm, tf), lambda bi, i, f: (layer, 0, nf + f)),
            pl.BlockSpec((None, 3, tf), lambda bi, i, f: (layer, 0, f)),
            pl.BlockSpec((None, 3, tf), lambda bi, i, f: (layer, 0, nf + f)),
            pl.BlockSpec((None, tf, dm), lambda bi, i, f: (layer, f, 0)),
        ],
        out_specs=pl.BlockSpec((None, tm, dm), lambda bi, i, f: (bi, i, 0)),
        out_shape=jax.ShapeDtypeStruct((b, s, dm), F32),
        scratch_shapes=[pltpu.VMEM((tm + 2 * NSEG, dm), BF16), pltpu.VMEM((3 * NSEG, dm), F32)],
        compiler_params=_cparams(("parallel", "parallel", "arbitrary")),
        name="ffn",
    )(x, x, x, g, wup, wup, cw, cw, wdown)


def _final_body(x_ref, g_ref, o_ref):
    o_ref[...] = _rms(x_ref[...], g_ref[...])


def _final_norm(x, g, tm):
    b, s, dm = x.shape
    tm = min(tm, s)
    return pl.pallas_call(
        _final_body,
        grid=(b, s // tm),
        in_specs=[pl.BlockSpec((None, tm, dm), lambda bi, i: (bi, i, 0)),
                  pl.BlockSpec((1, dm), lambda bi, i: (0, 0))],
        out_specs=pl.BlockSpec((None, tm, dm), lambda bi, i: (bi, i, 0)),
        out_shape=jax.ShapeDtypeStruct((b, s, dm), F32),
        compiler_params=_cparams(("parallel", "parallel")),
        name="final_norm",
    )(x, g)


def _block_diag_embed(w, nblk, spec_in, spec_out):
    return jnp.einsum(spec_in + ",gh->" + spec_out, w, jnp.eye(nblk, dtype=w.dtype))


def _s5_prep_body(tab_ref, bb_ref, cc_ref, m_ref, win_ref, woutt_ref, p_ref, q_ref, d_ref, *, npz):
    t = TCH
    ns = tab_ref.shape[-1]
    lane = lax.broadcasted_iota(jnp.int32, (S5_H, 2 * ns), 1)
    grp = (lane % ns) // npz
    nt = (((1,), (1,)), ((), ()))

    def spread(dst, compact):
        for g in range(GROUPS_PER_BLOCK):
            dst[g * S5_H:(g + 1) * S5_H, :] = jnp.where(grp == g, compact, 0.0)

    for d in range(2):
        br, bi = bb_ref[d, 0], bb_ref[d, 1]
        cr, ci = cc_ref[d, 0], cc_ref[d, 1]
        for k in range(t + 1):
            ar, ai = tab_ref[d, 0, k:k + 1, :], tab_ref[d, 1, k:k + 1, :]
            if k < t:
                spread(p_ref.at[d, k], jnp.concatenate([ar * br - ai * bi, ar * bi + ai * br], axis=1))
            spread(q_ref.at[d, k], jnp.concatenate([cr * ar - ci * ai, -(cr * ai + ci * ar)], axis=1))
        for k in range(t):
            pk, q0 = p_ref[d, k], q_ref[d, 0]
            ph, qh = pk.astype(BF16), q0.astype(BF16)
            pl_, ql = (pk - ph.astype(F32)).astype(BF16), (q0 - qh.astype(F32)).astype(BF16)
            d_ref[d, k] = (lax.dot_general(ph, qh, nt, preferred_element_type=F32)
                           + lax.dot_general(ph, ql, nt, preferred_element_type=F32)
                           + lax.dot_general(pl_, qh, nt, preferred_element_type=F32))

    for s in range(t):
        rows = slice(s * LANES, (s + 1) * LANES)
        for to in range(t):
            if to > s:
                blk = d_ref[0, to - s]
            elif to < s:
                blk = d_ref[1, s - to]
            else:
                blk = d_ref[0, 0] + d_ref[1, 0]
            m_ref[rows, to * LANES:(to + 1) * LANES] = blk.astype(BF16)
        win_ref[rows, 0:2 * ns] = p_ref[0, t - 1 - s].astype(BF16)
        win_ref[rows, 2 * ns:4 * ns] = p_ref[1, s].astype(BF16)
        woutt_ref[rows, 0:2 * ns] = q_ref[0, s + 1].astype(BF16)
        woutt_ref[rows, 2 * ns:4 * ns] = q_ref[1, t - s].astype(BF16)


def _prep_s5(a_re, a_im, log_dt, b_re, b_im, c_re, c_im, seg_lens):
    nl, _, ng, npz = a_re.shape
    gb = GROUPS_PER_BLOCK
    ncb = ng // gb
    ns = gb * npz
    kt = TCH * LANES
    lr, li = a_re.astype(F32), a_im.astype(F32)
    dt = jnp.exp(log_dt.astype(F32))[..., None]
    zr, zi = lr * dt, li * dt

    def cpow(k, re=zr, im=zi):
        mag = jnp.exp(re * k)
        return mag * jnp.cos(im * k), mag * jnp.sin(im * k)

    kv = jnp.arange(S5_H, dtype=F32)[:, None]
    tab = jnp.stack(cpow(kv, zr[..., None, :], zi[..., None, :]), axis=2)
    ar, ai = cpow(1.0)
    den = lr * lr + li * li
    fr = ((ar - 1.0) * lr + ai * li) / den
    fi = (ai * lr - (ar - 1.0) * li) / den
    b_re, b_im = b_re.astype(F32), b_im.astype(F32)
    bbr = fr[..., None] * b_re - fi[..., None] * b_im
    bbi = fr[..., None] * b_im + fi[..., None] * b_re
    bb = jnp.swapaxes(jnp.stack([bbr, bbi], axis=2), -1, -2)
    cc = jnp.stack([c_re.astype(F32), c_im.astype(F32)], axis=2)

    def lanes(v):
        v = v.reshape(nl, 2, 2, ncb, gb, S5_H, npz).transpose(0, 3, 1, 2, 5, 4, 6)
        return v.reshape(nl, ncb, 2, 2, S5_H, ns)

    tab_spec = pl.BlockSpec((None, None, 2, 2, S5_H, ns), lambda l, c: (l, c, 0, 0, 0, 0))
    w_spec = pl.BlockSpec((None, None, kt, 4 * ns), lambda l, c: (l, c, 0, 0))
    m, win, woutt = pl.pallas_call(
        functools.partial(_s5_prep_body, npz=npz),
        grid=(nl, ncb),
        in_specs=[tab_spec, tab_spec, tab_spec],
        out_specs=[pl.BlockSpec((None, None, kt, kt), lambda l, c: (l, c, 0, 0)), w_spec, w_spec],
        out_shape=[jax.ShapeDtypeStruct((nl, ncb, kt, kt), BF16),
                   jax.ShapeDtypeStruct((nl, ncb, kt, 4 * ns), BF16),
                   jax.ShapeDtypeStruct((nl, ncb, kt, 4 * ns), BF16)],
        scratch_shapes=[pltpu.VMEM((2, TCH, LANES, 2 * ns), F32),
                        pltpu.VMEM((2, TCH + 1, LANES, 2 * ns), F32),
                        pltpu.VMEM((2, TCH, LANES, LANES), F32)],
        compiler_params=_cparams(("parallel", "parallel")),
        name="s5_prep",
    )(lanes(tab), lanes(bb), lanes(cc))

    def vec(vr, vi):
        v = jnp.stack([vr[:, 0], vi[:, 0], vr[:, 1], vi[:, 1]], axis=1)
        return v.reshape(nl, 4, ncb, ns).transpose(0, 2, 1, 3)

    avs = [jnp.concatenate([vec(*cpow(float(TCH))), vec(*cpow(float(ls)))], axis=2) for ls in seg_lens]
    return m, win, woutt, avs


def _prep_lru(conv_b, w_a, b_a, w_x, b_x, lam):
    nl, _, nb, blk, _ = w_a.shape
    width = nb * blk
    ncb = width // LANES
    per = LANES // blk

    def bd(w):
        w = w.reshape(nl, ncb, per, blk, blk)
        return _block_diag_embed(w, per, "lcgij", "lcgihj").reshape(nl, ncb, LANES, LANES)

    wg = jnp.concatenate([bd(w_a[:, 0]), bd(w_x[:, 0]), bd(w_a[:, 1]), bd(w_x[:, 1])], axis=-1)
    rows = [conv_b, b_a[:, 0], b_x[:, 0], b_a[:, 1], b_x[:, 1], lam[:, 0], lam[:, 1],
            jnp.zeros_like(conv_b)]
    par = jnp.stack(rows, axis=1).astype(F32)
    par = par.reshape(nl, 8, ncb, LANES).transpose(0, 2, 1, 3)
    return (0.5 * wg).astype(BF16), par


def kernel(x_prompt, x_sample, norm1_g, w_in, s5_a_re, s5_a_im, s5_log_dt, s5_b_re, s5_b_im, s5_c_re, s5_c_im, s5_d, s5_w_glu, lru_conv_w, lru_conv_b, lru_w_a, lru_b_a, lru_w_x, lru_b_x, lru_lambda, w_proj_a, w_proj_b, w_out, norm2_g, ffn_w_up, ffn_conv_w, ffn_w_down, final_g):
    depth, dm, _ = w_in.shape
    w5 = s5_d.shape[-1]
    wl = lru_conv_b.shape[-1]
    groups = (x_prompt, x_sample)
    seg_lens = [x.shape[1] // NSEG for x in groups]

    w_in_p = w_in.astype(BF16)
    col_u = 0
    col_x = w5 // LANES
    col_g = col_x + wl // LANES
    m, win, wout, avs = _prep_s5(s5_a_re, s5_a_im, s5_log_dt, s5_b_re, s5_b_im, s5_c_re, s5_c_im, seg_lens)
    wg, par = _prep_lru(lru_conv_b, lru_w_a, lru_b_a, lru_w_x, lru_b_x, lru_lambda)
    conv_w = lru_conv_w.astype(F32)
    wglu, wa, wb, wo = (w.astype(BF16) for w in (s5_w_glu, w_proj_a, w_proj_b, w_out))
    wup, wdown = ffn_w_up.astype(BF16), ffn_w_down.astype(BF16)
    ffn_cw = ffn_conv_w.astype(F32)
    g1 = norm1_g.astype(F32)[:, None, :]
    g2 = norm2_g.astype(F32)[:, None, :]
    d5 = s5_d.astype(F32)[:, None, :]
    gfin = final_g.astype(F32)[None, :]

    outs = []
    for x, av in zip(groups, avs):
        x = _interleave(x.astype(F32))
        for l in range(depth):
            proj = _inproj(x, g1, w_in_p, l, tm=1024, tn=min(1024, w5))
            if w5 == wl and _mixers_vmem_bytes(x.shape[1]) <= VMEM_BYTES:
                ys, hb = _mixers(proj, col_u, col_x, m, win, wout, av, conv_w, par, wg, l, w5)
            else:
                ys = _s5_mix(proj, col_u, m, win, wout, av, l, w5)
                hb = _rglru(proj, col_x, conv_w, par, wg, l, wl)
            x = _postmix(x, ys, proj, hb, d5, wglu, wa, wb, wo, l, tm=256)
            x = _ffn(x, g2, wup, ffn_cw, wdown, l, tm=1024, tf=512)
        outs.append(_deinterleave(_final_norm(x, gfin, tm=512)))
    return tuple(outs)
```

```python
import functools
import math

import jax
import jax.numpy as jnp
from jax import lax
from jax.experimental import pallas as pl
from jax.experimental.pallas import tpu as pltpu

F32 = jnp.float32
BF16 = jnp.bfloat16

LANES = 128
MXU_COLS = 256
NSEG = 16
TCH = 8
S5_H = 16
GROUPS_PER_BLOCK = LANES // S5_H
EPS = 1e-6
LRU_C = 8.0
LOG2_E = 1.0 / math.log(2.0)
VMEM_BYTES = 64 * 1024 * 1024
VMEM_LIMIT = 60 * 1024 * 1024


def _cparams(sem, vmem=VMEM_LIMIT):
    return pltpu.CompilerParams(dimension_semantics=sem, vmem_limit_bytes=vmem)


def _resident(shape, imap):
    return pl.BlockSpec(shape, imap, pipeline_mode=pl.Buffered(1))


def _interleave(x):
    b, s, d = x.shape
    return x.reshape(b, NSEG, s // NSEG, d).transpose(0, 2, 1, 3).reshape(b, s, d)


def _deinterleave(x):
    b, s, d = x.shape
    return x.reshape(b, s // NSEG, NSEG, d).transpose(0, 2, 1, 3).reshape(b, s, d)


SCAN_UNROLL = 8


def _sigmoid(x):
    return 0.5 * jnp.tanh(0.5 * x) + 0.5


def _blk(i, n):
    return pl.ds(i * n, n) if isinstance(i, int) else pl.ds(pl.multiple_of(i * n, n), n)


def _rms(xf, g):
    return xf * lax.rsqrt(jnp.mean(xf * xf, axis=-1, keepdims=True) + EPS) * g


def _inproj_body(x_ref, g_ref, w_ref, o_ref, xn_ref):
    @pl.when(pl.program_id(2) == 0)
    def _():
        xn_ref[...] = _rms(x_ref[...], g_ref[...]).astype(BF16)

    o_ref[...] = jnp.dot(xn_ref[...], w_ref[...], preferred_element_type=F32).astype(o_ref.dtype)


def _inproj(x, g, w, layer, tm, tn):
    b, s, d = x.shape
    n = w.shape[-1]
    tm = min(tm, s)
    return pl.pallas_call(
        _inproj_body,
        grid=(b, s // tm, n // tn),
        in_specs=[
            pl.BlockSpec((None, tm, d), lambda bi, i, j: (bi, i, 0)),
            pl.BlockSpec((None, 1, d), lambda bi, i, j: (layer, 0, 0)),
            pl.BlockSpec((None, d, tn), lambda bi, i, j: (layer, 0, j)),
        ],
        out_specs=pl.BlockSpec((None, tm, tn), lambda bi, i, j: (bi, i, j)),
        out_shape=jax.ShapeDtypeStruct((b, s, n), BF16),
        scratch_shapes=[pltpu.VMEM((tm, d), BF16)],
        compiler_params=_cparams(("parallel", "parallel", "arbitrary")),
        name="inproj",
    )(x, g, w)


def _s5_body(u_ref, m_ref, win_ref, wout_ref, av_ref, o_ref,
             ucat_ref, s_ref, h_ref, e_ref, g_ref, *, nq, work=None):
    nc = nq * NSEG
    ns = 4 * LANES
    for t in range(TCH):
        ucat_ref[:, t * LANES:(t + 1) * LANES] = u_ref[:, t].reshape(nc, LANES)
    ucat = ucat_ref[...]
    work = list(work or ())
    nwc = max(min(len(work), win_ref.shape[-1] // MXU_COLS), 1)
    wc = win_ref.shape[-1] // nwc
    per_chunk = -(-len(work) // nwc)
    for k in range(nwc):
        cols = slice(k * wc, (k + 1) * wc)
        s_ref[:, cols] = jnp.dot(ucat, win_ref[:, cols], preferred_element_type=F32)
        for emit in work[k * per_chunk:(k + 1) * per_chunk]:
            emit()

    a_fr, a_fi = av_ref[0:1, :], av_ref[1:2, :]
    a_br, a_bi = av_ref[2:3, :], av_ref[3:4, :]

    def scan(init, store):
        def step(q, carry):
            hfr, hfi, hbr, hbi = carry
            rf = q * NSEG
            rb = (nq - 1 - q) * NSEG
            if store:
                h_ref[pl.ds(rf, NSEG), 0 * ns:1 * ns] = hfr.astype(BF16)
                h_ref[pl.ds(rf, NSEG), 1 * ns:2 * ns] = hfi.astype(BF16)
                h_ref[pl.ds(rb, NSEG), 2 * ns:3 * ns] = hbr.astype(BF16)
                h_ref[pl.ds(rb, NSEG), 3 * ns:4 * ns] = hbi.astype(BF16)
            sfr = s_ref[pl.ds(rf, NSEG), 0 * ns:1 * ns]
            sfi = s_ref[pl.ds(rf, NSEG), 1 * ns:2 * ns]
            sbr = s_ref[pl.ds(rb, NSEG), 2 * ns:3 * ns]
            sbi = s_ref[pl.ds(rb, NSEG), 3 * ns:4 * ns]
            return (a_fr * hfr - a_fi * hfi + sfr, a_fr * hfi + a_fi * hfr + sfi,
                    a_br * hbr - a_bi * hbi + sbr, a_br * hbi + a_bi * hbr + sbi)
        carry = init
        for q in range(nq):
            carry = step(q, carry)
        return carry

    zero = jnp.zeros((NSEG, ns), F32)
    efr, efi, ebr, ebi = scan((zero, zero, zero, zero), False)
    e_ref[:, 0 * ns:1 * ns] = efr
    e_ref[:, 1 * ns:2 * ns] = efi
    e_ref[:, 2 * ns:3 * ns] = ebr
    e_ref[:, 3 * ns:4 * ns] = ebi

    p_fr, p_fi = av_ref[4:5, :], av_ref[5:6, :]
    p_br, p_bi = av_ref[6:7, :], av_ref[7:8, :]
    zrow = jnp.zeros((1, ns), F32)
    g_ref[0:1, 0:2 * ns] = jnp.zeros((1, 2 * ns), F32)
    g_ref[NSEG - 1:NSEG, 2 * ns:4 * ns] = jnp.zeros((1, 2 * ns), F32)
    gr, gi = zrow, zrow
    for j in range(NSEG - 1):
        er, ei = e_ref[j:j + 1, 0 * ns:1 * ns], e_ref[j:j + 1, 1 * ns:2 * ns]
        gr, gi = p_fr * gr - p_fi * gi + er, p_fr * gi + p_fi * gr + ei
        g_ref[j + 1:j + 2, 0 * ns:1 * ns] = gr
        g_ref[j + 1:j + 2, 1 * ns:2 * ns] = gi
    gr, gi = zrow, zrow
    for j in range(NSEG - 1, 0, -1):
        er, ei = e_ref[j:j + 1, 2 * ns:3 * ns], e_ref[j:j + 1, 3 * ns:4 * ns]
        gr, gi = p_br * gr - p_bi * gi + er, p_br * gi + p_bi * gr + ei
        g_ref[j - 1:j, 2 * ns:3 * ns] = gr
        g_ref[j - 1:j, 3 * ns:4 * ns] = gi

    scan((g_ref[:, 0 * ns:1 * ns], g_ref[:, 1 * ns:2 * ns],
          g_ref[:, 2 * ns:3 * ns], g_ref[:, 3 * ns:4 * ns]), True)

    y = jnp.dot(ucat, m_ref[...], preferred_element_type=F32)
    y = y + lax.dot_general(h_ref[...], wout_ref[...], (((1,), (1,)), ((), ())),
                            preferred_element_type=F32)
    for t in range(TCH):
        o_ref[:, t] = y[:, t * LANES:(t + 1) * LANES].reshape(nq, NSEG, LANES).astype(o_ref.dtype)


def _s5_mix(proj, col0, m, win, wout, av, layer, width):
    b, s, n = proj.shape
    ncb = width // LANES
    nq = s // (NSEG * TCH)
    nc = nq * NSEG
    kt = TCH * LANES
    nst = 16 * LANES
    proj5 = proj.reshape(b, nq, TCH, NSEG, n)
    out = pl.pallas_call(
        functools.partial(_s5_body, nq=nq),
        grid=(ncb, b),
        in_specs=[
            pl.BlockSpec((None, nq, TCH, NSEG, LANES), lambda c, bi: (bi, 0, 0, 0, col0 + c)),
            _resident((None, None, kt, kt), lambda c, bi: (layer, c, 0, 0)),
            _resident((None, None, kt, nst), lambda c, bi: (layer, c, 0, 0)),
            _resident((None, None, kt, nst), lambda c, bi: (layer, c, 0, 0)),
            pl.BlockSpec((None, None, 8, 4 * LANES), lambda c, bi: (layer, c, 0, 0)),
        ],
        out_specs=pl.BlockSpec((None, nq, TCH, NSEG, LANES), lambda c, bi: (bi, 0, 0, 0, c)),
        out_shape=jax.ShapeDtypeStruct((b, nq, TCH, NSEG, width), BF16),
        scratch_shapes=[
            pltpu.VMEM((nc, kt), BF16),
            pltpu.VMEM((nc, nst), F32),
            pltpu.VMEM((nc, nst), BF16),
            pltpu.VMEM((NSEG, nst), F32),
            pltpu.VMEM((NSEG, nst), F32),
        ],
        compiler_params=_cparams(("parallel", "parallel")),
        name="s5_mix",
    )(proj5, m, win, wout, av)
    return out.reshape(b, s, width)


def _rglru_body(xr_ref, cw_ref, par_ref, wg_ref, o_ref,
                pad_ref, tmp_ref, af_ref, bf_ref, ab_ref, bb_ref, *, s, tt, schedule=None):
    r = NSEG
    ls = s // r
    pad_ref[2 * r:2 * r + s, :] = xr_ref[...].astype(F32)
    tmp_ref[...] = jnp.zeros((3 * r, LANES), F32)
    tmp_ref[r:2 * r, :] = xr_ref[s - 2 * r:s - r, :].astype(F32)
    pad_ref[0:r, :] = tmp_ref[r - 1:2 * r - 1, :]
    tmp_ref[r:2 * r, :] = xr_ref[s - r:s, :].astype(F32)
    pad_ref[r:2 * r, :] = tmp_ref[r - 1:2 * r - 1, :]
    tmp_ref[r:2 * r, :] = xr_ref[0:r, :].astype(F32)
    pad_ref[2 * r + s:3 * r + s, :] = tmp_ref[r + 1:2 * r + 1, :]

    cb = par_ref[0:1, :]
    hb_af, hb_xf, hb_ab, hb_xb = (0.5 * par_ref[k:k + 1, :] for k in range(1, 5))
    c2_f = (-0.5 * LRU_C * LOG2_E) * jax.nn.softplus(-par_ref[5:6, :])
    c2_b = (-0.5 * LRU_C * LOG2_E) * jax.nn.softplus(-par_ref[6:7, :])

    def gates(k, _):
        r0 = k * tt if isinstance(k, int) else pl.multiple_of(k * tt, tt)
        xc = cb + cw_ref[0:1, :] * pad_ref[pl.ds(r0, tt), :]
        for tap in range(1, 4):
            xc = xc + cw_ref[tap:tap + 1, :] * pad_ref[pl.ds(r0 + tap * r, tt), :]
        z = jnp.dot(xc.astype(BF16), wg_ref[...], preferred_element_type=F32)
        hx = 0.5 * xc
        for (a_ref, b_ref, c2, hba, hbx, off) in ((af_ref, bf_ref, c2_f, hb_af, hb_xf, 0),
                                                  (ab_ref, bb_ref, c2_b, hb_ab, hb_xb, 2)):
            tr = jnp.tanh(z[:, off * LANES:(off + 1) * LANES] + hba)
            tg = jnp.tanh(z[:, (off + 1) * LANES:(off + 2) * LANES] + hbx)
            a = jnp.exp2(c2 * tr + c2)
            v = 1.0 - a * a
            root = jnp.where(v > 0.0, v * lax.rsqrt(v), 0.0)
            a_ref[pl.ds(r0, tt), :] = a
            b_ref[pl.ds(r0, tt), :] = root * (hx * tg + hx)
        return 0

    def rows(i):
        return _blk(i, r)

    def pair(a_ref, b_ref, i0, i1):
        a0, b0, a1 = a_ref[rows(i0), :], b_ref[rows(i0), :], a_ref[rows(i1), :]
        return a0, b0, a1 * a0, a1 * b0 + b_ref[rows(i1), :]

    def local(k, carry):
        hf, pf, hb, pb = carry
        _, _, a2f, b2f = pair(af_ref, bf_ref, 2 * k, 2 * k + 1)
        _, _, a2b, b2b = pair(ab_ref, bb_ref, ls - 1 - 2 * k, ls - 2 - 2 * k)
        return a2f * hf + b2f, a2f * pf, a2b * hb + b2b, a2b * pb

    start = (jnp.zeros((r, LANES), F32), jnp.ones((r, LANES), F32)) * 2

    def carries(ef, pf, eb, pb):
        tmp_ref[0:r, :] = ef
        tmp_ref[r:2 * r, :] = pf
        g = jnp.zeros((1, LANES), F32)
        pad_ref[0:1, :] = g
        for j in range(r - 1):
            g = tmp_ref[r + j:r + j + 1, :] * g + tmp_ref[j:j + 1, :]
            pad_ref[j + 1:j + 2, :] = g
        gf = pad_ref[0:r, :]
        tmp_ref[0:r, :] = eb
        tmp_ref[r:2 * r, :] = pb
        g = jnp.zeros((1, LANES), F32)
        pad_ref[r - 1:r, :] = g
        for j in range(r - 1, 0, -1):
            g = tmp_ref[r + j:r + j + 1, :] * g + tmp_ref[j:j + 1, :]
            pad_ref[j - 1:j, :] = g
        return gf, pad_ref[0:r, :]

    def both(k, carry):
        hf, hb = carry
        i0, i1 = 2 * k, 2 * k + 1
        a0, b0, a2, b2 = pair(af_ref, bf_ref, i0, i1)
        bf_ref[rows(i0), :] = a0 * hf + b0
        hf = a2 * hf + b2
        bf_ref[rows(i1), :] = hf
        j0, j1 = ls - 1 - 2 * k, ls - 2 - 2 * k
        a0, b0, a2, b2 = pair(ab_ref, bb_ref, j0, j1)
        bb_ref[rows(j0), :] = a0 * hb + b0
        hb = a2 * hb + b2
        bb_ref[rows(j1), :] = hb
        return hf, hb

    def combine(k, _):
        rk = _blk(k, tt)
        o_ref[rk, :] = (bf_ref[rk, :] + bb_ref[rk, :]).astype(o_ref.dtype)
        return 0

    nt = s // tt
    if schedule is None:
        lax.fori_loop(0, nt, gates, 0)
    else:
        schedule([functools.partial(gates, k, 0) for k in range(nt)])
    ends = lax.fori_loop(0, ls // 2, local, start, unroll=SCAN_UNROLL)
    lax.fori_loop(0, ls // 2, both, carries(*ends), unroll=SCAN_UNROLL)
    lax.fori_loop(0, nt, combine, 0)


def _rglru(proj, col_x, conv_w, par, wg, layer, width):
    b, s, n = proj.shape
    ncb = width // LANES
    tt = min(512, s)
    return pl.pallas_call(
        functools.partial(_rglru_body, s=s, tt=tt),
        grid=(ncb, b),
        in_specs=[
            pl.BlockSpec((None, s, LANES), lambda c, bi: (bi, 0, col_x + c)),
            pl.BlockSpec((None, 4, LANES), lambda c, bi: (layer, 0, c)),
            pl.BlockSpec((None, None, 8, LANES), lambda c, bi: (layer, c, 0, 0)),
            pl.BlockSpec((None, None, LANES, 4 * LANES), lambda c, bi: (layer, c, 0, 0)),
        ],
        out_specs=pl.BlockSpec((None, s, LANES), lambda c, bi: (bi, 0, c)),
        out_shape=jax.ShapeDtypeStruct((b, s, width), BF16),
        scratch_shapes=[
            pltpu.VMEM((s + 3 * NSEG, LANES), F32),
            pltpu.VMEM((3 * NSEG, LANES), F32),
            pltpu.VMEM((s, LANES), F32),
            pltpu.VMEM((s, LANES), F32),
            pltpu.VMEM((s, LANES), F32),
            pltpu.VMEM((s, LANES), F32),
        ],
        compiler_params=_cparams(("parallel", "parallel")),
        name="rglru",
    )(proj, conv_w, par, wg)


def _mixers_body(u_ref, xr_ref, m_ref, win_ref, wout_ref, av_ref, cw_ref, par_ref, wg_ref,
                 ys_ref, hb_ref, ucat_ref, s_ref, h_ref, e_ref, g_ref,
                 pad_ref, tmp_ref, af_ref, bf_ref, ab_ref, bb_ref, *, nq, s, tt):
    def s5_with(items):
        _s5_body(u_ref, m_ref, win_ref, wout_ref, av_ref, ys_ref, ucat_ref, s_ref, h_ref, e_ref, g_ref,
                 nq=nq, work=items)

    _rglru_body(xr_ref, cw_ref, par_ref, wg_ref, hb_ref,
                pad_ref, tmp_ref, af_ref, bf_ref, ab_ref, bb_ref, s=s, tt=tt, schedule=s5_with)


def _mixers_vmem_bytes(s):
    kt, nst = TCH * LANES, 16 * LANES
    weights = 2 * (kt * kt + 2 * kt * nst)
    s5_rows = s * (4 * 2 * LANES) + (s // TCH) * (2 * kt + 4 * nst + 2 * nst)
    lru_rows = s * LANES * (2 * 2 * 2 + 5 * 4)
    return weights + s5_rows + lru_rows


def _mixers(proj, col_u, col_x, m, win, wout, av, conv_w, par, wg, layer, width):
    b, s, n = proj.shape
    ncb = width // LANES
    nq = s // (NSEG * TCH)
    nc = nq * NSEG
    kt = TCH * LANES
    nst = 16 * LANES
    tt = min(512, s)
    proj5 = proj.reshape(b, nq, TCH, NSEG, n)
    ys, hb = pl.pallas_call(
        functools.partial(_mixers_body, nq=nq, s=s, tt=tt),
        grid=(ncb, b),
        in_specs=[
            pl.BlockSpec((None, nq, TCH, NSEG, LANES), lambda c, bi: (bi, 0, 0, 0, col_u + c)),
            pl.BlockSpec((None, s, LANES), lambda c, bi: (bi, 0, col_x + c)),
            _resident((None, None, kt, kt), lambda c, bi: (layer, c, 0, 0)),
            _resident((None, None, kt, nst), lambda c, bi: (layer, c, 0, 0)),
            _resident((None, None, kt, nst), lambda c, bi: (layer, c, 0, 0)),
            pl.BlockSpec((None, None, 8, 4 * LANES), lambda c, bi: (layer, c, 0, 0)),
            pl.BlockSpec((None, 4, LANES), lambda c, bi: (layer, 0, c)),
            pl.BlockSpec((None, None, 8, LANES), lambda c, bi: (layer, c, 0, 0)),
            pl.BlockSpec((None, None, LANES, 4 * LANES), lambda c, bi: (layer, c, 0, 0)),
        ],
        out_specs=[pl.BlockSpec((None, nq, TCH, NSEG, LANES), lambda c, bi: (bi, 0, 0, 0, c)),
                   pl.BlockSpec((None, s, LANES), lambda c, bi: (bi, 0, c))],
        out_shape=[jax.ShapeDtypeStruct((b, nq, TCH, NSEG, width), BF16),
                   jax.ShapeDtypeStruct((b, s, width), BF16)],
        scratch_shapes=[
            pltpu.VMEM((nc, kt), BF16),
            pltpu.VMEM((nc, nst), F32),
            pltpu.VMEM((nc, nst), BF16),
            pltpu.VMEM((NSEG, nst), F32),
            pltpu.VMEM((NSEG, nst), F32),
            pltpu.VMEM((s + 3 * NSEG, LANES), F32),
            pltpu.VMEM((3 * NSEG, LANES), F32),
            pltpu.VMEM((s, LANES), F32),
            pltpu.VMEM((s, LANES), F32),
            pltpu.VMEM((s, LANES), F32),
            pltpu.VMEM((s, LANES), F32),
        ],
        compiler_params=_cparams(("parallel", "parallel")),
        name="mixers",
    )(proj5, proj, m, win, wout, av, conv_w, par, wg)
    return ys.reshape(b, s, width), hb


def _postmix_body(x_ref, ys_ref, u_ref, hb_ref, gr_ref, ga0_ref, ga1_ref, gb0_ref, gb1_ref, d_ref,
                  wglu_ref, wa_ref, wb_ref, wo_ref, o_ref):
    y = jax.nn.gelu(ys_ref[...].astype(F32) + d_ref[...] * u_ref[...].astype(F32))
    z = jnp.dot(y.astype(BF16), wglu_ref[...], preferred_element_type=F32)
    ya = (y * _sigmoid(z)).astype(BF16)
    pa = jnp.dot(ya, wa_ref[...], preferred_element_type=F32)
    yb = (hb_ref[...].astype(F32) * jax.nn.gelu(gr_ref[...].astype(F32))).astype(BF16)
    pb = jnp.dot(yb, wb_ref[...], preferred_element_type=F32)
    half = ga0_ref.shape[-1]
    merged = [(_sigmoid(ga_ref[...].astype(F32)) * pa[:, k * half:(k + 1) * half]
               + _sigmoid(gb_ref[...].astype(F32)) * pb[:, k * half:(k + 1) * half]).astype(BF16)
              for k, (ga_ref, gb_ref) in enumerate(((ga0_ref, gb0_ref), (ga1_ref, gb1_ref)))]
    o_ref[...] = x_ref[...] + jnp.dot(jnp.concatenate(merged, axis=1), wo_ref[...],
                                      preferred_element_type=F32)


def _postmix(x, ys, proj, yb, d, wglu, wa, wb, wo, layer, tm):
    b, s, dm = x.shape
    w5 = ys.shape[-1]
    wl = yb.shape[-1]
    tm = min(tm, s)
    half = dm // 2
    assert w5 == half and wl == half and proj.shape[-1] == 7 * half
    row = lambda bi, i: (bi, i, 0)
    gate = lambda k: pl.BlockSpec((None, tm, half), lambda bi, i: (bi, i, k))
    return pl.pallas_call(
        _postmix_body,
        grid=(b, s // tm),
        in_specs=[
            pl.BlockSpec((None, tm, dm), row),
            pl.BlockSpec((None, tm, w5), row),
            pl.BlockSpec((None, tm, w5), row),
            pl.BlockSpec((None, tm, wl), row),
            gate(2), gate(3), gate(4), gate(5), gate(6),
            pl.BlockSpec((None, 1, w5), lambda bi, i: (layer, 0, 0)),
            _resident((None, w5, w5), lambda bi, i: (layer, 0, 0)),
            _resident((None, w5, dm), lambda bi, i: (layer, 0, 0)),
            _resident((None, wl, dm), lambda bi, i: (layer, 0, 0)),
            _resident((None, dm, dm), lambda bi, i: (layer, 0, 0)),
        ],
        out_specs=pl.BlockSpec((None, tm, dm), row),
        out_shape=jax.ShapeDtypeStruct((b, s, dm), F32),
        compiler_params=_cparams(("parallel", "parallel")),
        name="postmix",
    )(x, ys, proj, yb, proj, proj, proj, proj, proj, d, wglu, wa, wb, wo)


def _ffn_body(x_ref, xp_ref, xq_ref, g_ref, wug_ref, wuv_ref, cwg_ref, cwv_ref, wd_ref, gout_ref, o_ref,
              xn_ref, edge_ref, *, tm, norm_out):
    r = NSEG
    i = pl.program_id(1)

    @pl.when(pl.program_id(2) == 0)
    def _():
        g = g_ref[...]
        xn_ref[r:r + tm, :] = _rms(x_ref[...], g).astype(BF16)
        edge_ref[...] = jnp.zeros(edge_ref.shape, F32)
        prev = _rms(xp_ref[...], g)
        edge_ref[r:2 * r, :] = prev
        prev = jnp.where(i == 0, edge_ref[r - 1:2 * r - 1, :], prev)
        xn_ref[0:r, :] = prev.astype(BF16)
        nxt = _rms(xq_ref[...], g)
        edge_ref[r:2 * r, :] = nxt
        nxt = jnp.where(i == pl.num_programs(1) - 1, edge_ref[r + 1:2 * r + 1, :], nxt)
        xn_ref[r + tm:2 * r + tm, :] = nxt.astype(BF16)
        o_ref[...] = x_ref[...]

    xn = xn_ref[...]

    def conv(w_ref, cw_ref):
        h = jnp.dot(xn, w_ref[...], preferred_element_type=F32)
        return (cw_ref[0:1, :] * h[0:tm] + cw_ref[1:2, :] * h[r:r + tm]
                + cw_ref[2:3, :] * h[2 * r:2 * r + tm])

    act = (jax.nn.gelu(conv(wug_ref, cwg_ref)) * conv(wuv_ref, cwv_ref)).astype(BF16)
    o_ref[...] += jnp.dot(act, wd_ref[...], preferred_element_type=F32)

    if norm_out:
        @pl.when(pl.program_id(2) == pl.num_programs(2) - 1)
        def _():
            o_ref[...] = _rms(o_ref[...], gout_ref[...])


def _ffn(x, g, wup, cw, wdown, gout, layer, tm, tf, norm_out):
    b, s, dm = x.shape
    fh = wdown.shape[1]
    tm = min(tm, s)
    tf = min(tf, fh)
    nf = fh // tf
    gpt = tm // NSEG
    ng = s // NSEG
    return pl.pallas_call(
        functools.partial(_ffn_body, tm=tm, norm_out=norm_out),
        grid=(b, s // tm, nf),
        in_specs=[
            _resident((None, tm, dm), lambda bi, i, f: (bi, i, 0)),
            pl.BlockSpec((None, NSEG, dm), lambda bi, i, f: (bi, lax.rem(i * gpt + ng - 1, ng), 0)),
            pl.BlockSpec((None, NSEG, dm), lambda bi, i, f: (bi, lax.rem((i + 1) * gpt, ng), 0)),
            pl.BlockSpec((None, 1, dm), lambda bi, i, f: (layer, 0, 0)),
            pl.BlockSpec((None, dm, tf), lambda bi, i, f: (layer, 0, f)),
            pl.BlockSpec((None, dm, tf), lambda bi, i, f: (layer, 0, nf + f)),
            pl.BlockSpec((None, 3, tf), lambda bi, i, f: (layer, 0, f)),
            pl.BlockSpec((None, 3, tf), lambda bi, i, f: (layer, 0, nf + f)),
            pl.BlockSpec((None, tf, dm), lambda bi, i, f: (layer, f, 0)),
            pl.BlockSpec((1, dm), lambda bi, i, f: (0, 0)),
        ],
        out_specs=pl.BlockSpec((None, tm, dm), lambda bi, i, f: (bi, i, 0)),
        out_shape=jax.ShapeDtypeStruct((b, s, dm), F32),
        scratch_shapes=[pltpu.VMEM((tm + 2 * NSEG, dm), BF16), pltpu.VMEM((3 * NSEG, dm), F32)],
        compiler_params=_cparams(("parallel", "parallel", "arbitrary")),
        name="ffn",
    )(x, x, x, g, wup, wup, cw, cw, wdown, gout)


def _block_diag_embed(w, nblk, spec_in, spec_out):
    return jnp.einsum(spec_in + ",gh->" + spec_out, w, jnp.eye(nblk, dtype=w.dtype))


def _s5_prep_body(tab_ref, bb_ref, cc_ref, m_ref, win_ref, woutt_ref, p_ref, q_ref, d_ref, *, npz):
    t = TCH
    ns = tab_ref.shape[-1]
    lane = lax.broadcasted_iota(jnp.int32, (S5_H, 2 * ns), 1)
    grp = (lane % ns) // npz
    nt = (((1,), (1,)), ((), ()))

    def spread(dst, compact):
        for g in range(GROUPS_PER_BLOCK):
            dst[g * S5_H:(g + 1) * S5_H, :] = jnp.where(grp == g, compact, 0.0)

    for d in range(2):
        br, bi = bb_ref[d, 0], bb_ref[d, 1]
        cr, ci = cc_ref[d, 0], cc_ref[d, 1]
        for k in range(t + 1):
            ar, ai = tab_ref[d, 0, k:k + 1, :], tab_ref[d, 1, k:k + 1, :]
            if k < t:
                spread(p_ref.at[d, k], jnp.concatenate([ar * br - ai * bi, ar * bi + ai * br], axis=1))
            spread(q_ref.at[d, k], jnp.concatenate([cr * ar - ci * ai, -(cr * ai + ci * ar)], axis=1))
        pk, q0 = p_ref[d].reshape(t * LANES, 2 * ns), q_ref[d, 0]
        ph, qh = pk.astype(BF16), q0.astype(BF16)
        pl_, ql = (pk - ph.astype(F32)).astype(BF16), (q0 - qh.astype(F32)).astype(BF16)
        taps = (lax.dot_general(ph, qh, nt, preferred_element_type=F32)
                + lax.dot_general(ph, ql, nt, preferred_element_type=F32)
                + lax.dot_general(pl_, qh, nt, preferred_element_type=F32))
        d_ref[d] = taps.reshape(t, LANES, LANES)

    for s in range(t):
        rows = slice(s * LANES, (s + 1) * LANES)
        for to in range(t):
            if to > s:
                blk = d_ref[0, to - s]
            elif to < s:
                blk = d_ref[1, s - to]
            else:
                blk = d_ref[0, 0] + d_ref[1, 0]
            m_ref[rows, to * LANES:(to + 1) * LANES] = blk.astype(BF16)
        win_ref[rows, 0:2 * ns] = p_ref[0, t - 1 - s].astype(BF16)
        win_ref[rows, 2 * ns:4 * ns] = p_ref[1, s].astype(BF16)
        woutt_ref[rows, 0:2 * ns] = q_ref[0, s + 1].astype(BF16)
        woutt_ref[rows, 2 * ns:4 * ns] = q_ref[1, t - s].astype(BF16)


def _prep_s5(a_re, a_im, log_dt, b_re, b_im, c_re, c_im, seg_lens):
    nl, _, ng, npz = a_re.shape
    gb = GROUPS_PER_BLOCK
    ncb = ng // gb
    ns = gb * npz
    kt = TCH * LANES
    lr, li = a_re.astype(F32), a_im.astype(F32)
    dt = jnp.exp(log_dt.astype(F32))[..., None]
    zr, zi = lr * dt, li * dt

    def cpow(k, re=zr, im=zi):
        mag = jnp.exp(re * k)
        return mag * jnp.cos(im * k), mag * jnp.sin(im * k)

    kv = jnp.arange(S5_H, dtype=F32)[:, None]
    tab = jnp.stack(cpow(kv, zr[..., None, :], zi[..., None, :]), axis=2)
    ar, ai = cpow(1.0)
    den = lr * lr + li * li
    fr = ((ar - 1.0) * lr + ai * li) / den
    fi = (ai * lr - (ar - 1.0) * li) / den
    b_re, b_im = b_re.astype(F32), b_im.astype(F32)
    bbr = fr[..., None] * b_re - fi[..., None] * b_im
    bbi = fr[..., None] * b_im + fi[..., None] * b_re
    bb = jnp.swapaxes(jnp.stack([bbr, bbi], axis=2), -1, -2)
    cc = jnp.stack([c_re.astype(F32), c_im.astype(F32)], axis=2)

    def lanes(v):
        v = v.reshape(nl, 2, 2, ncb, gb, S5_H, npz).transpose(0, 3, 1, 2, 5, 4, 6)
        return v.reshape(nl, ncb, 2, 2, S5_H, ns)

    tab_spec = pl.BlockSpec((None, None, 2, 2, S5_H, ns), lambda l, c: (l, c, 0, 0, 0, 0))
    w_spec = pl.BlockSpec((None, None, kt, 4 * ns), lambda l, c: (l, c, 0, 0))
    m, win, woutt = pl.pallas_call(
        functools.partial(_s5_prep_body, npz=npz),
        grid=(nl, ncb),
        in_specs=[tab_spec, tab_spec, tab_spec],
        out_specs=[pl.BlockSpec((None, None, kt, kt), lambda l, c: (l, c, 0, 0)), w_spec, w_spec],
        out_shape=[jax.ShapeDtypeStruct((nl, ncb, kt, kt), BF16),
                   jax.ShapeDtypeStruct((nl, ncb, kt, 4 * ns), BF16),
                   jax.ShapeDtypeStruct((nl, ncb, kt, 4 * ns), BF16)],
        scratch_shapes=[pltpu.VMEM((2, TCH, LANES, 2 * ns), F32),
                        pltpu.VMEM((2, TCH + 1, LANES, 2 * ns), F32),
                        pltpu.VMEM((2, TCH, LANES, LANES), F32)],
        compiler_params=_cparams(("parallel", "parallel")),
        name="s5_prep",
    )(lanes(tab), lanes(bb), lanes(cc))

    def vec(vr, vi):
        v = jnp.stack([vr[:, 0], vi[:, 0], vr[:, 1], vi[:, 1]], axis=1)
        return v.reshape(nl, 4, ncb, ns).transpose(0, 2, 1, 3)

    avs = [jnp.concatenate([vec(*cpow(float(TCH))), vec(*cpow(float(ls)))], axis=2) for ls in seg_lens]
    return m, win, woutt, avs


def _prep_lru(conv_b, w_a, b_a, w_x, b_x, lam):
    nl, _, nb, blk, _ = w_a.shape
    width = nb * blk
    ncb = width // LANES
    per = LANES // blk

    def bd(w):
        w = w.reshape(nl, ncb, per, blk, blk)
        return _block_diag_embed(w, per, "lcgij", "lcgihj").reshape(nl, ncb, LANES, LANES)

    wg = jnp.concatenate([bd(w_a[:, 0]), bd(w_x[:, 0]), bd(w_a[:, 1]), bd(w_x[:, 1])], axis=-1)
    rows = [conv_b, b_a[:, 0], b_x[:, 0], b_a[:, 1], b_x[:, 1], lam[:, 0], lam[:, 1],
            jnp.zeros_like(conv_b)]
    par = jnp.stack(rows, axis=1).astype(F32)
    par = par.reshape(nl, 8, ncb, LANES).transpose(0, 2, 1, 3)
    return (0.5 * wg).astype(BF16), par


def kernel(x_prompt, x_sample, norm1_g, w_in, s5_a_re, s5_a_im, s5_log_dt, s5_b_re, s5_b_im, s5_c_re, s5_c_im, s5_d, s5_w_glu, lru_conv_w, lru_conv_b, lru_w_a, lru_b_a, lru_w_x, lru_b_x, lru_lambda, w_proj_a, w_proj_b, w_out, norm2_g, ffn_w_up, ffn_conv_w, ffn_w_down, final_g):
    depth, dm, _ = w_in.shape
    w5 = s5_d.shape[-1]
    wl = lru_conv_b.shape[-1]
    groups = (x_prompt, x_sample)
    seg_lens = [x.shape[1] // NSEG for x in groups]

    w_in_p = w_in.astype(BF16)
    col_u = 0
    col_x = w5 // LANES
    m, win, wout, avs = _prep_s5(s5_a_re, s5_a_im, s5_log_dt, s5_b_re, s5_b_im, s5_c_re, s5_c_im, seg_lens)
    wg, par = _prep_lru(lru_conv_b, lru_w_a, lru_b_a, lru_w_x, lru_b_x, lru_lambda)
    conv_w = lru_conv_w.astype(F32)
    wglu, wa, wb, wo = (w.astype(BF16) for w in (s5_w_glu, w_proj_a, w_proj_b, w_out))
    wup, wdown = ffn_w_up.astype(BF16), ffn_w_down.astype(BF16)
    ffn_cw = ffn_conv_w.astype(F32)
    g1 = norm1_g.astype(F32)[:, None, :]
    g2 = norm2_g.astype(F32)[:, None, :]
    d5 = s5_d.astype(F32)[:, None, :]
    gfin = final_g.astype(F32)[None, :]

    outs = []
    for x, av in zip(groups, avs):
        x = _interleave(x.astype(F32))
        for l in range(depth):
            proj = _inproj(x, g1, w_in_p, l, tm=1024, tn=min(1024, w5))
            if w5 == wl and _mixers_vmem_bytes(x.shape[1]) <= VMEM_BYTES:
                ys, hb = _mixers(proj, col_u, col_x, m, win, wout, av, conv_w, par, wg, l, w5)
            else:
                ys = _s5_mix(proj, col_u, m, win, wout, av, l, w5)
                hb = _rglru(proj, col_x, conv_w, par, wg, l, wl)
            x = _postmix(x, ys, proj, hb, d5, wglu, wa, wb, wo, l, tm=256)
            x = _ffn(x, g2, wup, ffn_cw, wdown, gfin, l, tm=1024, tf=512, norm_out=(l == depth - 1))
        outs.append(_deinterleave(x))
    return tuple(outs)
```

```python
import functools
import math

import jax
import jax.numpy as jnp
from jax import lax
from jax.experimental import pallas as pl
from jax.experimental.pallas import tpu as pltpu

F32 = jnp.float32
BF16 = jnp.bfloat16

LANES = 128
MXU_COLS = 256
INPROJ_ROWS, INPROJ_COLS = 1024, 1792
POSTMIX_ROWS = 256
FFN_ROWS, FFN_COLS = 1024, 512
GATE_ROWS = 512
NSEG = 16
TCH = 8
S5_H = 16
GROUPS_PER_BLOCK = LANES // S5_H
EPS = 1e-6
LRU_C = 8.0
LOG2_E = 1.0 / math.log(2.0)
VMEM_BYTES = 64 * 1024 * 1024
VMEM_LIMIT = 60 * 1024 * 1024


def _cparams(sem, vmem=VMEM_LIMIT):
    return pltpu.CompilerParams(dimension_semantics=sem, vmem_limit_bytes=vmem)


def _resident(shape, imap):
    return pl.BlockSpec(shape, imap, pipeline_mode=pl.Buffered(1))


def _interleave(x):
    b, s, d = x.shape
    return x.reshape(b, NSEG, s // NSEG, d).transpose(0, 2, 1, 3).reshape(b, s, d)


def _deinterleave(x):
    b, s, d = x.shape
    return x.reshape(b, s // NSEG, NSEG, d).transpose(0, 2, 1, 3).reshape(b, s, d)


SCAN_UNROLL = 8


def _sigmoid(x):
    return 0.5 * jnp.tanh(0.5 * x) + 0.5


def _blk(i, n):
    return pl.ds(i * n, n) if isinstance(i, int) else pl.ds(pl.multiple_of(i * n, n), n)


def _rms(xf, g):
    return xf * lax.rsqrt(jnp.mean(xf * xf, axis=-1, keepdims=True) + EPS) * g


def _inproj_body(x_ref, g_ref, w_ref, o_ref, xn_ref):
    @pl.when(pl.program_id(2) == 0)
    def _():
        xn_ref[...] = _rms(x_ref[...], g_ref[...]).astype(BF16)

    o_ref[...] = jnp.dot(xn_ref[...], w_ref[...], preferred_element_type=F32).astype(o_ref.dtype)


def _inproj(x, g, w, layer):
    b, s, d = x.shape
    n = w.shape[-1]
    tm = min(INPROJ_ROWS, s)
    tn = max(c for c in range(LANES, min(INPROJ_COLS, n) + 1, LANES) if n % c == 0)
    return pl.pallas_call(
        _inproj_body,
        grid=(b, s // tm, n // tn),
        in_specs=[
            pl.BlockSpec((None, tm, d), lambda bi, i, j: (bi, i, 0)),
            pl.BlockSpec((None, 1, d), lambda bi, i, j: (layer, 0, 0)),
            pl.BlockSpec((None, d, tn), lambda bi, i, j: (layer, 0, j)),
        ],
        out_specs=pl.BlockSpec((None, tm, tn), lambda bi, i, j: (bi, i, j)),
        out_shape=jax.ShapeDtypeStruct((b, s, n), BF16),
        scratch_shapes=[pltpu.VMEM((tm, d), BF16)],
        compiler_params=_cparams(("parallel", "parallel", "arbitrary")),
        name="inproj",
    )(x, g, w)


def _s5_body(u_ref, m_ref, win_ref, wout_ref, av_ref, o_ref,
             ucat_ref, s_ref, h_ref, e_ref, g_ref, *, nq, work):
    nc = nq * NSEG
    ns = 4 * LANES
    for t in range(TCH):
        ucat_ref[:, t * LANES:(t + 1) * LANES] = u_ref[:, t].reshape(nc, LANES)
    ucat = ucat_ref[...]
    nwc = min(len(work), win_ref.shape[-1] // MXU_COLS)
    wc = win_ref.shape[-1] // nwc
    per_chunk = -(-len(work) // nwc)
    for k in range(nwc):
        cols = slice(k * wc, (k + 1) * wc)
        s_ref[:, cols] = jnp.dot(ucat, win_ref[:, cols], preferred_element_type=F32)
        for emit in work[k * per_chunk:(k + 1) * per_chunk]:
            emit()

    a_fr, a_fi = av_ref[0:1, :], av_ref[1:2, :]
    a_br, a_bi = av_ref[2:3, :], av_ref[3:4, :]

    def scan(init, store):
        def step(q, carry):
            hfr, hfi, hbr, hbi = carry
            rf = q * NSEG
            rb = (nq - 1 - q) * NSEG
            if store:
                h_ref[pl.ds(rf, NSEG), 0 * ns:1 * ns] = hfr.astype(BF16)
                h_ref[pl.ds(rf, NSEG), 1 * ns:2 * ns] = hfi.astype(BF16)
                h_ref[pl.ds(rb, NSEG), 2 * ns:3 * ns] = hbr.astype(BF16)
                h_ref[pl.ds(rb, NSEG), 3 * ns:4 * ns] = hbi.astype(BF16)
            sfr = s_ref[pl.ds(rf, NSEG), 0 * ns:1 * ns]
            sfi = s_ref[pl.ds(rf, NSEG), 1 * ns:2 * ns]
            sbr = s_ref[pl.ds(rb, NSEG), 2 * ns:3 * ns]
            sbi = s_ref[pl.ds(rb, NSEG), 3 * ns:4 * ns]
            return (a_fr * hfr - a_fi * hfi + sfr, a_fr * hfi + a_fi * hfr + sfi,
                    a_br * hbr - a_bi * hbi + sbr, a_br * hbi + a_bi * hbr + sbi)
        carry = init
        for q in range(nq):
            carry = step(q, carry)
        return carry

    zero = jnp.zeros((NSEG, ns), F32)
    efr, efi, ebr, ebi = scan((zero, zero, zero, zero), False)
    e_ref[:, 0 * ns:1 * ns] = efr
    e_ref[:, 1 * ns:2 * ns] = efi
    e_ref[:, 2 * ns:3 * ns] = ebr
    e_ref[:, 3 * ns:4 * ns] = ebi

    p_fr, p_fi = av_ref[4:5, :], av_ref[5:6, :]
    p_br, p_bi = av_ref[6:7, :], av_ref[7:8, :]
    zrow = jnp.zeros((1, ns), F32)
    g_ref[0:1, 0:2 * ns] = jnp.zeros((1, 2 * ns), F32)
    g_ref[NSEG - 1:NSEG, 2 * ns:4 * ns] = jnp.zeros((1, 2 * ns), F32)
    gr, gi = zrow, zrow
    for j in range(NSEG - 1):
        er, ei = e_ref[j:j + 1, 0 * ns:1 * ns], e_ref[j:j + 1, 1 * ns:2 * ns]
        gr, gi = p_fr * gr - p_fi * gi + er, p_fr * gi + p_fi * gr + ei
        g_ref[j + 1:j + 2, 0 * ns:1 * ns] = gr
        g_ref[j + 1:j + 2, 1 * ns:2 * ns] = gi
    gr, gi = zrow, zrow
    for j in range(NSEG - 1, 0, -1):
        er, ei = e_ref[j:j + 1, 2 * ns:3 * ns], e_ref[j:j + 1, 3 * ns:4 * ns]
        gr, gi = p_br * gr - p_bi * gi + er, p_br * gi + p_bi * gr + ei
        g_ref[j - 1:j, 2 * ns:3 * ns] = gr
        g_ref[j - 1:j, 3 * ns:4 * ns] = gi

    scan((g_ref[:, 0 * ns:1 * ns], g_ref[:, 1 * ns:2 * ns],
          g_ref[:, 2 * ns:3 * ns], g_ref[:, 3 * ns:4 * ns]), True)

    y = jnp.dot(ucat, m_ref[...], preferred_element_type=F32)
    y = y + lax.dot_general(h_ref[...], wout_ref[...], (((1,), (1,)), ((), ())),
                            preferred_element_type=F32)
    for t in range(TCH):
        o_ref[:, t] = y[:, t * LANES:(t + 1) * LANES].reshape(nq, NSEG, LANES).astype(o_ref.dtype)


def _rglru_body(xr_ref, cw_ref, par_ref, wg_ref, o_ref,
                pad_ref, tmp_ref, af_ref, bf_ref, ab_ref, bb_ref, *, s, tt, schedule):
    r = NSEG
    ls = s // r
    pad_ref[2 * r:2 * r + s, :] = xr_ref[...].astype(F32)
    tmp_ref[...] = jnp.zeros((3 * r, LANES), F32)
    tmp_ref[r:2 * r, :] = xr_ref[s - 2 * r:s - r, :].astype(F32)
    pad_ref[0:r, :] = tmp_ref[r - 1:2 * r - 1, :]
    tmp_ref[r:2 * r, :] = xr_ref[s - r:s, :].astype(F32)
    pad_ref[r:2 * r, :] = tmp_ref[r - 1:2 * r - 1, :]
    tmp_ref[r:2 * r, :] = xr_ref[0:r, :].astype(F32)
    pad_ref[2 * r + s:3 * r + s, :] = tmp_ref[r + 1:2 * r + 1, :]

    cb = par_ref[0:1, :]
    hb_af, hb_xf, hb_ab, hb_xb = (0.5 * par_ref[k:k + 1, :] for k in range(1, 5))
    c2_f = (-0.5 * LRU_C * LOG2_E) * jax.nn.softplus(-par_ref[5:6, :])
    c2_b = (-0.5 * LRU_C * LOG2_E) * jax.nn.softplus(-par_ref[6:7, :])

    def gates(k, _):
        r0 = k * tt if isinstance(k, int) else pl.multiple_of(k * tt, tt)
        xc = cb + cw_ref[0:1, :] * pad_ref[pl.ds(r0, tt), :]
        for tap in range(1, 4):
            xc = xc + cw_ref[tap:tap + 1, :] * pad_ref[pl.ds(r0 + tap * r, tt), :]
        z = jnp.dot(xc.astype(BF16), wg_ref[...], preferred_element_type=F32)
        hx = 0.5 * xc
        for (a_ref, b_ref, c2, hba, hbx, off) in ((af_ref, bf_ref, c2_f, hb_af, hb_xf, 0),
                                                  (ab_ref, bb_ref, c2_b, hb_ab, hb_xb, 2)):
            tr = jnp.tanh(z[:, off * LANES:(off + 1) * LANES] + hba)
            tg = jnp.tanh(z[:, (off + 1) * LANES:(off + 2) * LANES] + hbx)
            a = jnp.exp2(c2 * tr + c2)
            v = 1.0 - a * a
            root = jnp.where(v > 0.0, v * lax.rsqrt(v), 0.0)
            a_ref[pl.ds(r0, tt), :] = a
            b_ref[pl.ds(r0, tt), :] = root * (hx * tg + hx)
        return 0

    def rows(i):
        return _blk(i, r)

    def pair(a_ref, b_ref, i0, i1):
        a0, b0, a1 = a_ref[rows(i0), :], b_ref[rows(i0), :], a_ref[rows(i1), :]
        return a0, b0, a1 * a0, a1 * b0 + b_ref[rows(i1), :]

    def local(k, carry):
        hf, pf, hb, pb = carry
        _, _, a2f, b2f = pair(af_ref, bf_ref, 2 * k, 2 * k + 1)
        _, _, a2b, b2b = pair(ab_ref, bb_ref, ls - 1 - 2 * k, ls - 2 - 2 * k)
        return a2f * hf + b2f, a2f * pf, a2b * hb + b2b, a2b * pb

    start = (jnp.zeros((r, LANES), F32), jnp.ones((r, LANES), F32)) * 2

    def carries(ef, pf, eb, pb):
        tmp_ref[0:r, :] = ef
        tmp_ref[r:2 * r, :] = pf
        g = jnp.zeros((1, LANES), F32)
        pad_ref[0:1, :] = g
        for j in range(r - 1):
            g = tmp_ref[r + j:r + j + 1, :] * g + tmp_ref[j:j + 1, :]
            pad_ref[j + 1:j + 2, :] = g
        gf = pad_ref[0:r, :]
        tmp_ref[0:r, :] = eb
        tmp_ref[r:2 * r, :] = pb
        g = jnp.zeros((1, LANES), F32)
        pad_ref[r - 1:r, :] = g
        for j in range(r - 1, 0, -1):
            g = tmp_ref[r + j:r + j + 1, :] * g + tmp_ref[j:j + 1, :]
            pad_ref[j - 1:j, :] = g
        return gf, pad_ref[0:r, :]

    def both(k, carry):
        hf, hb = carry
        i0, i1 = 2 * k, 2 * k + 1
        a0, b0, a2, b2 = pair(af_ref, bf_ref, i0, i1)
        bf_ref[rows(i0), :] = a0 * hf + b0
        hf = a2 * hf + b2
        bf_ref[rows(i1), :] = hf
        j0, j1 = ls - 1 - 2 * k, ls - 2 - 2 * k
        a0, b0, a2, b2 = pair(ab_ref, bb_ref, j0, j1)
        bb_ref[rows(j0), :] = a0 * hb + b0
        hb = a2 * hb + b2
        bb_ref[rows(j1), :] = hb
        return hf, hb

    def combine(k, _):
        rk = _blk(k, tt)
        o_ref[rk, :] = (bf_ref[rk, :] + bb_ref[rk, :]).astype(o_ref.dtype)
        return 0

    nt = s // tt
    schedule([functools.partial(gates, k, 0) for k in range(nt)])
    ends = lax.fori_loop(0, ls // 2, local, start, unroll=SCAN_UNROLL)
    lax.fori_loop(0, ls // 2, both, carries(*ends), unroll=SCAN_UNROLL)
    lax.fori_loop(0, nt, combine, 0)


def _mixers_body(u_ref, xr_ref, m_ref, win_ref, wout_ref, av_ref, cw_ref, par_ref, wg_ref,
                 ys_ref, hb_ref, ucat_ref, s_ref, h_ref, e_ref, g_ref,
                 pad_ref, tmp_ref, af_ref, bf_ref, ab_ref, bb_ref, *, nq, s, tt):
    def s5_with(items):
        _s5_body(u_ref, m_ref, win_ref, wout_ref, av_ref, ys_ref, ucat_ref, s_ref, h_ref, e_ref, g_ref,
                 nq=nq, work=items)

    _rglru_body(xr_ref, cw_ref, par_ref, wg_ref, hb_ref,
                pad_ref, tmp_ref, af_ref, bf_ref, ab_ref, bb_ref, s=s, tt=tt, schedule=s5_with)


def _mixers_vmem_bytes(s):
    kt, nst = TCH * LANES, 16 * LANES
    weights = 2 * (kt * kt + 2 * kt * nst)
    s5_rows = s * (4 * 2 * LANES) + (s // TCH) * (2 * kt + 4 * nst + 2 * nst)
    lru_rows = s * LANES * (2 * 2 * 2 + 5 * 4)
    return weights + s5_rows + lru_rows


def _mixers(proj, col_u, col_x, m, win, wout, av, conv_w, par, wg, layer, width):
    b, s, n = proj.shape
    ncb = width // LANES
    nq = s // (NSEG * TCH)
    nc = nq * NSEG
    kt = TCH * LANES
    nst = 16 * LANES
    tt = min(GATE_ROWS, s)
    assert _mixers_vmem_bytes(s) <= VMEM_BYTES, "sequence too long for one VMEM-resident channel block"
    proj5 = proj.reshape(b, nq, TCH, NSEG, n)
    ys, hb = pl.pallas_call(
        functools.partial(_mixers_body, nq=nq, s=s, tt=tt),
        grid=(ncb, b),
        in_specs=[
            pl.BlockSpec((None, nq, TCH, NSEG, LANES), lambda c, bi: (bi, 0, 0, 0, col_u + c)),
            pl.BlockSpec((None, s, LANES), lambda c, bi: (bi, 0, col_x + c)),
            _resident((None, None, kt, kt), lambda c, bi: (layer, c, 0, 0)),
            _resident((None, None, kt, nst), lambda c, bi: (layer, c, 0, 0)),
            _resident((None, None, kt, nst), lambda c, bi: (layer, c, 0, 0)),
            pl.BlockSpec((None, None, 8, 4 * LANES), lambda c, bi: (layer, c, 0, 0)),
            pl.BlockSpec((None, 4, LANES), lambda c, bi: (layer, 0, c)),
            pl.BlockSpec((None, None, 8, LANES), lambda c, bi: (layer, c, 0, 0)),
            pl.BlockSpec((None, None, LANES, 4 * LANES), lambda c, bi: (layer, c, 0, 0)),
        ],
        out_specs=[pl.BlockSpec((None, nq, TCH, NSEG, LANES), lambda c, bi: (bi, 0, 0, 0, c)),
                   pl.BlockSpec((None, s, LANES), lambda c, bi: (bi, 0, c))],
        out_shape=[jax.ShapeDtypeStruct((b, nq, TCH, NSEG, width), BF16),
                   jax.ShapeDtypeStruct((b, s, width), BF16)],
        scratch_shapes=[
            pltpu.VMEM((nc, kt), BF16),
            pltpu.VMEM((nc, nst), F32),
            pltpu.VMEM((nc, nst), BF16),
            pltpu.VMEM((NSEG, nst), F32),
            pltpu.VMEM((NSEG, nst), F32),
            pltpu.VMEM((s + 3 * NSEG, LANES), F32),
            pltpu.VMEM((3 * NSEG, LANES), F32),
            pltpu.VMEM((s, LANES), F32),
            pltpu.VMEM((s, LANES), F32),
            pltpu.VMEM((s, LANES), F32),
            pltpu.VMEM((s, LANES), F32),
        ],
        compiler_params=_cparams(("parallel", "parallel")),
        name="mixers",
    )(proj5, proj, m, win, wout, av, conv_w, par, wg)
    return ys.reshape(b, s, width), hb


def _postmix_body(x_ref, ys_ref, u_ref, hb_ref, gr_ref, ga0_ref, ga1_ref, gb0_ref, gb1_ref, d_ref,
                  wglu_ref, wa_ref, wb_ref, wo_ref, o_ref):
    y = jax.nn.gelu(ys_ref[...].astype(F32) + d_ref[...] * u_ref[...].astype(F32))
    z = jnp.dot(y.astype(BF16), wglu_ref[...], preferred_element_type=F32)
    ya = (y * _sigmoid(z)).astype(BF16)
    pa = jnp.dot(ya, wa_ref[...], preferred_element_type=F32)
    yb = (hb_ref[...].astype(F32) * jax.nn.gelu(gr_ref[...].astype(F32))).astype(BF16)
    pb = jnp.dot(yb, wb_ref[...], preferred_element_type=F32)
    half = ga0_ref.shape[-1]
    merged = [(_sigmoid(ga_ref[...].astype(F32)) * pa[:, k * half:(k + 1) * half]
               + _sigmoid(gb_ref[...].astype(F32)) * pb[:, k * half:(k + 1) * half]).astype(BF16)
              for k, (ga_ref, gb_ref) in enumerate(((ga0_ref, gb0_ref), (ga1_ref, gb1_ref)))]
    o_ref[...] = x_ref[...] + jnp.dot(jnp.concatenate(merged, axis=1), wo_ref[...],
                                      preferred_element_type=F32)


def _postmix(x, ys, proj, yb, d, wglu, wa, wb, wo, layer):
    b, s, dm = x.shape
    w5 = ys.shape[-1]
    wl = yb.shape[-1]
    tm = min(POSTMIX_ROWS, s)
    half = dm // 2
    assert w5 == half and wl == half and proj.shape[-1] == 7 * half
    row = lambda bi, i: (bi, i, 0)
    gate = lambda k: pl.BlockSpec((None, tm, half), lambda bi, i: (bi, i, k))
    return pl.pallas_call(
        _postmix_body,
        grid=(b, s // tm),
        in_specs=[
            pl.BlockSpec((None, tm, dm), row),
            pl.BlockSpec((None, tm, w5), row),
            pl.BlockSpec((None, tm, w5), row),
            pl.BlockSpec((None, tm, wl), row),
            gate(2), gate(3), gate(4), gate(5), gate(6),
            pl.BlockSpec((None, 1, w5), lambda bi, i: (layer, 0, 0)),
            _resident((None, w5, w5), lambda bi, i: (layer, 0, 0)),
            _resident((None, w5, dm), lambda bi, i: (layer, 0, 0)),
            _resident((None, wl, dm), lambda bi, i: (layer, 0, 0)),
            _resident((None, dm, dm), lambda bi, i: (layer, 0, 0)),
        ],
        out_specs=pl.BlockSpec((None, tm, dm), row),
        out_shape=jax.ShapeDtypeStruct((b, s, dm), F32),
        compiler_params=_cparams(("parallel", "parallel")),
        name="postmix",
    )(x, ys, proj, yb, proj, proj, proj, proj, proj, d, wglu, wa, wb, wo)


def _ffn_body(x_ref, xp_ref, xq_ref, g_ref, wug_ref, wuv_ref, cwg_ref, cwv_ref, wd_ref, gout_ref, o_ref,
              xn_ref, edge_ref, *, tm, norm_out):
    r = NSEG
    i = pl.program_id(1)

    @pl.when(pl.program_id(2) == 0)
    def _():
        g = g_ref[...]
        xn_ref[r:r + tm, :] = _rms(x_ref[...], g).astype(BF16)
        edge_ref[...] = jnp.zeros(edge_ref.shape, F32)
        prev = _rms(xp_ref[...], g)
        edge_ref[r:2 * r, :] = prev
        prev = jnp.where(i == 0, edge_ref[r - 1:2 * r - 1, :], prev)
        xn_ref[0:r, :] = prev.astype(BF16)
        nxt = _rms(xq_ref[...], g)
        edge_ref[r:2 * r, :] = nxt
        nxt = jnp.where(i == pl.num_programs(1) - 1, edge_ref[r + 1:2 * r + 1, :], nxt)
        xn_ref[r + tm:2 * r + tm, :] = nxt.astype(BF16)
        o_ref[...] = x_ref[...]

    xn = xn_ref[...]

    def conv(w_ref, cw_ref):
        h = jnp.dot(xn, w_ref[...], preferred_element_type=F32)
        return (cw_ref[0:1, :] * h[0:tm] + cw_ref[1:2, :] * h[r:r + tm]
                + cw_ref[2:3, :] * h[2 * r:2 * r + tm])

    act = (jax.nn.gelu(conv(wug_ref, cwg_ref)) * conv(wuv_ref, cwv_ref)).astype(BF16)
    o_ref[...] += jnp.dot(act, wd_ref[...], preferred_element_type=F32)

    if norm_out:
        @pl.when(pl.program_id(2) == pl.num_programs(2) - 1)
        def _():
            o_ref[...] = _rms(o_ref[...], gout_ref[...])


def _ffn(x, g, wup, cw, wdown, gout, layer, norm_out):
    b, s, dm = x.shape
    fh = wdown.shape[1]
    tm = min(FFN_ROWS, s)
    tf = min(FFN_COLS, fh)
    nf = fh // tf
    gpt = tm // NSEG
    ng = s // NSEG
    return pl.pallas_call(
        functools.partial(_ffn_body, tm=tm, norm_out=norm_out),
        grid=(b, s // tm, nf),
        in_specs=[
            _resident((None, tm, dm), lambda bi, i, f: (bi, i, 0)),
            pl.BlockSpec((None, NSEG, dm), lambda bi, i, f: (bi, lax.rem(i * gpt + ng - 1, ng), 0)),
            pl.BlockSpec((None, NSEG, dm), lambda bi, i, f: (bi, lax.rem((i + 1) * gpt, ng), 0)),
            pl.BlockSpec((None, 1, dm), lambda bi, i, f: (layer, 0, 0)),
            pl.BlockSpec((None, dm, tf), lambda bi, i, f: (layer, 0, f)),
            pl.BlockSpec((None, dm, tf), lambda bi, i, f: (layer, 0, nf + f)),
            pl.BlockSpec((None, 3, tf), lambda bi, i, f: (layer, 0, f)),
            pl.BlockSpec((None, 3, tf), lambda bi, i, f: (layer, 0, nf + f)),
            pl.BlockSpec((None, tf, dm), lambda bi, i, f: (layer, f, 0)),
            pl.BlockSpec((1, dm), lambda bi, i, f: (0, 0)),
        ],
        out_specs=pl.BlockSpec((None, tm, dm), lambda bi, i, f: (bi, i, 0)),
        out_shape=jax.ShapeDtypeStruct((b, s, dm), F32),
        scratch_shapes=[pltpu.VMEM((tm + 2 * NSEG, dm), BF16), pltpu.VMEM((3 * NSEG, dm), F32)],
        compiler_params=_cparams(("parallel", "parallel", "arbitrary")),
        name="ffn",
    )(x, x, x, g, wup, wup, cw, cw, wdown, gout)


def _block_diag_embed(w, nblk, spec_in, spec_out):
    return jnp.einsum(spec_in + ",gh->" + spec_out, w, jnp.eye(nblk, dtype=w.dtype))


def _s5_prep_body(tab_ref, bb_ref, cc_ref, m_ref, win_ref, woutt_ref, p_ref, q_ref, d_ref, *, npz):
    t = TCH
    ns = tab_ref.shape[-1]
    lane = lax.broadcasted_iota(jnp.int32, (S5_H, 2 * ns), 1)
    grp = (lane % ns) // npz
    nt = (((1,), (1,)), ((), ()))

    def spread(dst, compact):
        for g in range(GROUPS_PER_BLOCK):
            dst[g * S5_H:(g + 1) * S5_H, :] = jnp.where(grp == g, compact, 0.0)

    for d in range(2):
        br, bi = bb_ref[d, 0], bb_ref[d, 1]
        cr, ci = cc_ref[d, 0], cc_ref[d, 1]
        for k in range(t + 1):
            ar, ai = tab_ref[d, 0, k:k + 1, :], tab_ref[d, 1, k:k + 1, :]
            if k < t:
                spread(p_ref.at[d, k], jnp.concatenate([ar * br - ai * bi, ar * bi + ai * br], axis=1))
            spread(q_ref.at[d, k], jnp.concatenate([cr * ar - ci * ai, -(cr * ai + ci * ar)], axis=1))
        pk, q0 = p_ref[d].reshape(t * LANES, 2 * ns), q_ref[d, 0]
        ph, qh = pk.astype(BF16), q0.astype(BF16)
        pl_, ql = (pk - ph.astype(F32)).astype(BF16), (q0 - qh.astype(F32)).astype(BF16)
        taps = (lax.dot_general(ph, qh, nt, preferred_element_type=F32)
                + lax.dot_general(ph, ql, nt, preferred_element_type=F32)
                + lax.dot_general(pl_, qh, nt, preferred_element_type=F32))
        d_ref[d] = taps.reshape(t, LANES, LANES)

    for s in range(t):
        rows = slice(s * LANES, (s + 1) * LANES)
        for to in range(t):
            if to > s:
                blk = d_ref[0, to - s]
            elif to < s:
                blk = d_ref[1, s - to]
            else:
                blk = d_ref[0, 0] + d_ref[1, 0]
            m_ref[rows, to * LANES:(to + 1) * LANES] = blk.astype(BF16)
        win_ref[rows, 0:2 * ns] = p_ref[0, t - 1 - s].astype(BF16)
        win_ref[rows, 2 * ns:4 * ns] = p_ref[1, s].astype(BF16)
        woutt_ref[rows, 0:2 * ns] = q_ref[0, s + 1].astype(BF16)
        woutt_ref[rows, 2 * ns:4 * ns] = q_ref[1, t - s].astype(BF16)


def _prep_s5(a_re, a_im, log_dt, b_re, b_im, c_re, c_im, seg_lens):
    nl, _, ng, npz = a_re.shape
    gb = GROUPS_PER_BLOCK
    ncb = ng // gb
    ns = gb * npz
    kt = TCH * LANES
    lr, li = a_re.astype(F32), a_im.astype(F32)
    dt = jnp.exp(log_dt.astype(F32))[..., None]
    zr, zi = lr * dt, li * dt

    def cpow(k, re=zr, im=zi):
        mag = jnp.exp(re * k)
        return mag * jnp.cos(im * k), mag * jnp.sin(im * k)

    kv = jnp.arange(S5_H, dtype=F32)[:, None]
    tab = jnp.stack(cpow(kv, zr[..., None, :], zi[..., None, :]), axis=2)
    ar, ai = cpow(1.0)
    den = lr * lr + li * li
    fr = ((ar - 1.0) * lr + ai * li) / den
    fi = (ai * lr - (ar - 1.0) * li) / den
    b_re, b_im = b_re.astype(F32), b_im.astype(F32)
    bbr = fr[..., None] * b_re - fi[..., None] * b_im
    bbi = fr[..., None] * b_im + fi[..., None] * b_re
    bb = jnp.swapaxes(jnp.stack([bbr, bbi], axis=2), -1, -2)
    cc = jnp.stack([c_re.astype(F32), c_im.astype(F32)], axis=2)

    def lanes(v):
        v = v.reshape(nl, 2, 2, ncb, gb, S5_H, npz).transpose(0, 3, 1, 2, 5, 4, 6)
        return v.reshape(nl, ncb, 2, 2, S5_H, ns)

    tab_spec = pl.BlockSpec((None, None, 2, 2, S5_H, ns), lambda l, c: (l, c, 0, 0, 0, 0))
    w_spec = pl.BlockSpec((None, None, kt, 4 * ns), lambda l, c: (l, c, 0, 0))
    m, win, woutt = pl.pallas_call(
        functools.partial(_s5_prep_body, npz=npz),
        grid=(nl, ncb),
        in_specs=[tab_spec, tab_spec, tab_spec],
        out_specs=[pl.BlockSpec((None, None, kt, kt), lambda l, c: (l, c, 0, 0)), w_spec, w_spec],
        out_shape=[jax.ShapeDtypeStruct((nl, ncb, kt, kt), BF16),
                   jax.ShapeDtypeStruct((nl, ncb, kt, 4 * ns), BF16),
                   jax.ShapeDtypeStruct((nl, ncb, kt, 4 * ns), BF16)],
        scratch_shapes=[pltpu.VMEM((2, TCH, LANES, 2 * ns), F32),
                        pltpu.VMEM((2, TCH + 1, LANES, 2 * ns), F32),
                        pltpu.VMEM((2, TCH, LANES, LANES), F32)],
        compiler_params=_cparams(("parallel", "parallel")),
        name="s5_prep",
    )(lanes(tab), lanes(bb), lanes(cc))

    def vec(vr, vi):
        v = jnp.stack([vr[:, 0], vi[:, 0], vr[:, 1], vi[:, 1]], axis=1)
        return v.reshape(nl, 4, ncb, ns).transpose(0, 2, 1, 3)

    avs = [jnp.concatenate([vec(*cpow(float(TCH))), vec(*cpow(float(ls)))], axis=2) for ls in seg_lens]
    return m, win, woutt, avs


def _prep_lru(conv_b, w_a, b_a, w_x, b_x, lam):
    nl, _, nb, blk, _ = w_a.shape
    width = nb * blk
    ncb = width // LANES
    per = LANES // blk

    def bd(w):
        w = w.reshape(nl, ncb, per, blk, blk)
        return _block_diag_embed(w, per, "lcgij", "lcgihj").reshape(nl, ncb, LANES, LANES)

    wg = jnp.concatenate([bd(w_a[:, 0]), bd(w_x[:, 0]), bd(w_a[:, 1]), bd(w_x[:, 1])], axis=-1)
    rows = [conv_b, b_a[:, 0], b_x[:, 0], b_a[:, 1], b_x[:, 1], lam[:, 0], lam[:, 1],
            jnp.zeros_like(conv_b)]
    par = jnp.stack(rows, axis=1).astype(F32)
    par = par.reshape(nl, 8, ncb, LANES).transpose(0, 2, 1, 3)
    return (0.5 * wg).astype(BF16), par


def kernel(x_prompt, x_sample, norm1_g, w_in, s5_a_re, s5_a_im, s5_log_dt, s5_b_re, s5_b_im, s5_c_re, s5_c_im, s5_d, s5_w_glu, lru_conv_w, lru_conv_b, lru_w_a, lru_b_a, lru_w_x, lru_b_x, lru_lambda, w_proj_a, w_proj_b, w_out, norm2_g, ffn_w_up, ffn_conv_w, ffn_w_down, final_g):
    depth, dm, _ = w_in.shape
    w5 = s5_d.shape[-1]
    wl = lru_conv_b.shape[-1]
    groups = (x_prompt, x_sample)
    seg_lens = [x.shape[1] // NSEG for x in groups]

    w_in_p = w_in.astype(BF16)
    col_u = 0
    col_x = w5 // LANES
    m, win, wout, avs = _prep_s5(s5_a_re, s5_a_im, s5_log_dt, s5_b_re, s5_b_im, s5_c_re, s5_c_im, seg_lens)
    wg, par = _prep_lru(lru_conv_b, lru_w_a, lru_b_a, lru_w_x, lru_b_x, lru_lambda)
    conv_w = lru_conv_w.astype(F32)
    wglu, wa, wb, wo = (w.astype(BF16) for w in (s5_w_glu, w_proj_a, w_proj_b, w_out))
    wup, wdown = ffn_w_up.astype(BF16), ffn_w_down.astype(BF16)
    ffn_cw = ffn_conv_w.astype(F32)
    g1 = norm1_g.astype(F32)[:, None, :]
    g2 = norm2_g.astype(F32)[:, None, :]
    d5 = s5_d.astype(F32)[:, None, :]
    gfin = final_g.astype(F32)[None, :]

    outs = []
    for x, av in zip(groups, avs):
        x = _interleave(x.astype(F32))
        for l in range(depth):
            proj = _inproj(x, g1, w_in_p, l)
            ys, hb = _mixers(proj, col_u, col_x, m, win, wout, av, conv_w, par, wg, l, w5)
            x = _postmix(x, ys, proj, hb, d5, wglu, wa, wb, wo, l)
            x = _ffn(x, g2, wup, ffn_cw, wdown, gfin, l, norm_out=(l == depth - 1))
        outs.append(_deinterleave(x))
    return tuple(outs)
```

```python
import functools
import math

import jax
import jax.numpy as jnp
from jax import lax
from jax.experimental import pallas as pl
from jax.experimental.pallas import tpu as pltpu

F32 = jnp.float32
BF16 = jnp.bfloat16

LANES = 128
MXU_COLS = 256
INPROJ_ROWS, INPROJ_COLS = 1024, 1792
POSTMIX_ROWS = 256
FFN_ROWS, FFN_COLS = 1024, 512
GATE_ROWS = 512
NSEG = 16
TCH = 8
S5_H = 16
GROUPS_PER_BLOCK = LANES // S5_H
EPS = 1e-6
LRU_C = 8.0
LOG2_E = 1.0 / math.log(2.0)
VMEM_BYTES = 64 * 1024 * 1024
VMEM_LIMIT = 60 * 1024 * 1024


def _cparams(sem, vmem=VMEM_LIMIT):
    return pltpu.CompilerParams(dimension_semantics=sem, vmem_limit_bytes=vmem)


def _resident(shape, imap):
    return pl.BlockSpec(shape, imap, pipeline_mode=pl.Buffered(1))


def _interleave(x):
    b, s, d = x.shape
    return x.reshape(b, NSEG, s // NSEG, d).transpose(0, 2, 1, 3).reshape(b, s, d)


def _deinterleave(x):
    b, s, d = x.shape
    return x.reshape(b, s // NSEG, NSEG, d).transpose(0, 2, 1, 3).reshape(b, s, d)


SCAN_UNROLL = 8


def _sigmoid(x):
    return 0.5 * jnp.tanh(0.5 * x) + 0.5


def _blk(i, n):
    return pl.ds(i * n, n) if isinstance(i, int) else pl.ds(pl.multiple_of(i * n, n), n)


def _rms(xf, g):
    return xf * lax.rsqrt(jnp.mean(xf * xf, axis=-1, keepdims=True) + EPS) * g


def _inproj_body(x_ref, g_ref, w_ref, o_ref, xn_ref):
    def project():
        o_ref[...] = jnp.dot(xn_ref[...], w_ref[...], preferred_element_type=F32).astype(o_ref.dtype)

    @pl.when(pl.program_id(2) == 0)
    def _():
        xn_ref[...] = _rms(x_ref[...], g_ref[...]).astype(BF16)
        project()

    @pl.when(pl.program_id(2) > 0)
    def _():
        project()


def _inproj(x, g, w, layer):
    b, s, d = x.shape
    n = w.shape[-1]
    tm = min(INPROJ_ROWS, s)
    tn = max(c for c in range(LANES, min(INPROJ_COLS, n) + 1, LANES) if n % c == 0)
    return pl.pallas_call(
        _inproj_body,
        grid=(b, s // tm, n // tn),
        in_specs=[
            pl.BlockSpec((None, tm, d), lambda bi, i, j: (bi, i, 0)),
            pl.BlockSpec((None, 1, d), lambda bi, i, j: (layer, 0, 0)),
            pl.BlockSpec((None, d, tn), lambda bi, i, j: (layer, 0, j)),
        ],
        out_specs=pl.BlockSpec((None, tm, tn), lambda bi, i, j: (bi, i, j)),
        out_shape=jax.ShapeDtypeStruct((b, s, n), BF16),
        scratch_shapes=[pltpu.VMEM((tm, d), BF16)],
        compiler_params=_cparams(("parallel", "parallel", "arbitrary")),
        name="inproj",
    )(x, g, w)


def _s5_body(u_ref, m_ref, win_ref, wout_ref, av_ref, o_ref,
             ucat_ref, s_ref, h_ref, e_ref, g_ref, *, nq, work):
    nc = nq * NSEG
    ns = 4 * LANES
    for t in range(TCH):
        ucat_ref[:, t * LANES:(t + 1) * LANES] = u_ref[:, t].reshape(nc, LANES)
    ucat = ucat_ref[...]
    nwc = min(len(work), win_ref.shape[-1] // MXU_COLS)
    wc = win_ref.shape[-1] // nwc
    per_chunk = -(-len(work) // nwc)
    for k in range(nwc):
        cols = slice(k * wc, (k + 1) * wc)
        s_ref[:, cols] = jnp.dot(ucat, win_ref[:, cols], preferred_element_type=F32)
        for emit in work[k * per_chunk:(k + 1) * per_chunk]:
            emit()

    a_fr, a_fi = av_ref[0:1, :], av_ref[1:2, :]
    a_br, a_bi = av_ref[2:3, :], av_ref[3:4, :]

    def scan(init, store):
        def step(q, carry):
            hfr, hfi, hbr, hbi = carry
            rf = q * NSEG
            rb = (nq - 1 - q) * NSEG
            if store:
                h_ref[pl.ds(rf, NSEG), 0 * ns:1 * ns] = hfr.astype(BF16)
                h_ref[pl.ds(rf, NSEG), 1 * ns:2 * ns] = hfi.astype(BF16)
                h_ref[pl.ds(rb, NSEG), 2 * ns:3 * ns] = hbr.astype(BF16)
                h_ref[pl.ds(rb, NSEG), 3 * ns:4 * ns] = hbi.astype(BF16)
            sfr = s_ref[pl.ds(rf, NSEG), 0 * ns:1 * ns]
            sfi = s_ref[pl.ds(rf, NSEG), 1 * ns:2 * ns]
            sbr = s_ref[pl.ds(rb, NSEG), 2 * ns:3 * ns]
            sbi = s_ref[pl.ds(rb, NSEG), 3 * ns:4 * ns]
            return (a_fr * hfr - a_fi * hfi + sfr, a_fr * hfi + a_fi * hfr + sfi,
                    a_br * hbr - a_bi * hbi + sbr, a_br * hbi + a_bi * hbr + sbi)
        carry = init
        for q in range(nq):
            carry = step(q, carry)
        return carry

    zero = jnp.zeros((NSEG, ns), F32)
    efr, efi, ebr, ebi = scan((zero, zero, zero, zero), False)
    e_ref[:, 0 * ns:1 * ns] = efr
    e_ref[:, 1 * ns:2 * ns] = efi
    e_ref[:, 2 * ns:3 * ns] = ebr
    e_ref[:, 3 * ns:4 * ns] = ebi

    p_fr, p_fi = av_ref[4:5, :], av_ref[5:6, :]
    p_br, p_bi = av_ref[6:7, :], av_ref[7:8, :]
    zrow = jnp.zeros((1, ns), F32)
    g_ref[0:1, 0:2 * ns] = jnp.zeros((1, 2 * ns), F32)
    g_ref[NSEG - 1:NSEG, 2 * ns:4 * ns] = jnp.zeros((1, 2 * ns), F32)
    gr, gi = zrow, zrow
    for j in range(NSEG - 1):
        er, ei = e_ref[j:j + 1, 0 * ns:1 * ns], e_ref[j:j + 1, 1 * ns:2 * ns]
        gr, gi = p_fr * gr - p_fi * gi + er, p_fr * gi + p_fi * gr + ei
        g_ref[j + 1:j + 2, 0 * ns:1 * ns] = gr
        g_ref[j + 1:j + 2, 1 * ns:2 * ns] = gi
    gr, gi = zrow, zrow
    for j in range(NSEG - 1, 0, -1):
        er, ei = e_ref[j:j + 1, 2 * ns:3 * ns], e_ref[j:j + 1, 3 * ns:4 * ns]
        gr, gi = p_br * gr - p_bi * gi + er, p_br * gi + p_bi * gr + ei
        g_ref[j - 1:j, 2 * ns:3 * ns] = gr
        g_ref[j - 1:j, 3 * ns:4 * ns] = gi

    scan((g_ref[:, 0 * ns:1 * ns], g_ref[:, 1 * ns:2 * ns],
          g_ref[:, 2 * ns:3 * ns], g_ref[:, 3 * ns:4 * ns]), True)

    y = jnp.dot(ucat, m_ref[...], preferred_element_type=F32)
    y = y + lax.dot_general(h_ref[...], wout_ref[...], (((1,), (1,)), ((), ())),
                            preferred_element_type=F32)
    for t in range(TCH):
        o_ref[:, t] = y[:, t * LANES:(t + 1) * LANES].reshape(nq, NSEG, LANES).astype(o_ref.dtype)


def _rglru_body(xr_ref, cw_ref, par_ref, wg_ref, o_ref,
                pad_ref, tmp_ref, af_ref, bf_ref, ab_ref, bb_ref, *, s, tt, schedule):
    r = NSEG
    ls = s // r
    pad_ref[2 * r:2 * r + s, :] = xr_ref[...].astype(F32)
    tmp_ref[...] = jnp.zeros((3 * r, LANES), F32)
    tmp_ref[r:2 * r, :] = xr_ref[s - 2 * r:s - r, :].astype(F32)
    pad_ref[0:r, :] = tmp_ref[r - 1:2 * r - 1, :]
    tmp_ref[r:2 * r, :] = xr_ref[s - r:s, :].astype(F32)
    pad_ref[r:2 * r, :] = tmp_ref[r - 1:2 * r - 1, :]
    tmp_ref[r:2 * r, :] = xr_ref[0:r, :].astype(F32)
    pad_ref[2 * r + s:3 * r + s, :] = tmp_ref[r + 1:2 * r + 1, :]

    cb = par_ref[0:1, :]
    hb_af, hb_xf, hb_ab, hb_xb = (0.5 * par_ref[k:k + 1, :] for k in range(1, 5))
    c2_f = (-0.5 * LRU_C * LOG2_E) * jax.nn.softplus(-par_ref[5:6, :])
    c2_b = (-0.5 * LRU_C * LOG2_E) * jax.nn.softplus(-par_ref[6:7, :])

    def gates(k, _):
        r0 = k * tt if isinstance(k, int) else pl.multiple_of(k * tt, tt)
        xc = cb + cw_ref[0:1, :] * pad_ref[pl.ds(r0, tt), :]
        for tap in range(1, 4):
            xc = xc + cw_ref[tap:tap + 1, :] * pad_ref[pl.ds(r0 + tap * r, tt), :]
        z = jnp.dot(xc.astype(BF16), wg_ref[...], preferred_element_type=F32)
        hx = 0.5 * xc
        for (a_ref, b_ref, c2, hba, hbx, off) in ((af_ref, bf_ref, c2_f, hb_af, hb_xf, 0),
                                                  (ab_ref, bb_ref, c2_b, hb_ab, hb_xb, 2)):
            tr = jnp.tanh(z[:, off * LANES:(off + 1) * LANES] + hba)
            tg = jnp.tanh(z[:, (off + 1) * LANES:(off + 2) * LANES] + hbx)
            a = jnp.exp2(c2 * tr + c2)
            v = 1.0 - a * a
            root = jnp.where(v > 0.0, v * lax.rsqrt(v), 0.0)
            a_ref[pl.ds(r0, tt), :] = a
            b_ref[pl.ds(r0, tt), :] = root * (hx * tg + hx)
        return 0

    def rows(i):
        return _blk(i, r)

    def pair(a_ref, b_ref, i0, i1):
        a0, b0, a1 = a_ref[rows(i0), :], b_ref[rows(i0), :], a_ref[rows(i1), :]
        return a0, b0, a1 * a0, a1 * b0 + b_ref[rows(i1), :]

    def local(k, carry):
        hf, pf, hb, pb = carry
        _, _, a2f, b2f = pair(af_ref, bf_ref, 2 * k, 2 * k + 1)
        _, _, a2b, b2b = pair(ab_ref, bb_ref, ls - 1 - 2 * k, ls - 2 - 2 * k)
        return a2f * hf + b2f, a2f * pf, a2b * hb + b2b, a2b * pb

    start = (jnp.zeros((r, LANES), F32), jnp.ones((r, LANES), F32)) * 2

    def carries(ef, pf, eb, pb):
        tmp_ref[0:r, :] = ef
        tmp_ref[r:2 * r, :] = pf
        g = jnp.zeros((1, LANES), F32)
        pad_ref[0:1, :] = g
        for j in range(r - 1):
            g = tmp_ref[r + j:r + j + 1, :] * g + tmp_ref[j:j + 1, :]
            pad_ref[j + 1:j + 2, :] = g
        gf = pad_ref[0:r, :]
        tmp_ref[0:r, :] = eb
        tmp_ref[r:2 * r, :] = pb
        g = jnp.zeros((1, LANES), F32)
        pad_ref[r - 1:r, :] = g
        for j in range(r - 1, 0, -1):
            g = tmp_ref[r + j:r + j + 1, :] * g + tmp_ref[j:j + 1, :]
            pad_ref[j - 1:j, :] = g
        return gf, pad_ref[0:r, :]

    def both(k, carry):
        hf, hb = carry
        i0, i1 = 2 * k, 2 * k + 1
        a0, b0, a2, b2 = pair(af_ref, bf_ref, i0, i1)
        bf_ref[rows(i0), :] = a0 * hf + b0
        hf = a2 * hf + b2
        bf_ref[rows(i1), :] = hf
        j0, j1 = ls - 1 - 2 * k, ls - 2 - 2 * k
        a0, b0, a2, b2 = pair(ab_ref, bb_ref, j0, j1)
        bb_ref[rows(j0), :] = a0 * hb + b0
        hb = a2 * hb + b2
        bb_ref[rows(j1), :] = hb
        return hf, hb

    def combine(k, _):
        rk = _blk(k, tt)
        o_ref[rk, :] = (bf_ref[rk, :] + bb_ref[rk, :]).astype(o_ref.dtype)
        return 0

    nt = s // tt
    schedule([functools.partial(gates, k, 0) for k in range(nt)])
    ends = lax.fori_loop(0, ls // 2, local, start, unroll=SCAN_UNROLL)
    lax.fori_loop(0, ls // 2, both, carries(*ends), unroll=SCAN_UNROLL)
    lax.fori_loop(0, nt, combine, 0)


def _mixers_body(u_ref, xr_ref, m_ref, win_ref, wout_ref, av_ref, cw_ref, par_ref, wg_ref,
                 ys_ref, hb_ref, ucat_ref, s_ref, h_ref, e_ref, g_ref,
                 pad_ref, tmp_ref, af_ref, bf_ref, ab_ref, bb_ref, *, nq, s, tt):
    def s5_with(items):
        _s5_body(u_ref, m_ref, win_ref, wout_ref, av_ref, ys_ref, ucat_ref, s_ref, h_ref, e_ref, g_ref,
                 nq=nq, work=items)

    _rglru_body(xr_ref, cw_ref, par_ref, wg_ref, hb_ref,
                pad_ref, tmp_ref, af_ref, bf_ref, ab_ref, bb_ref, s=s, tt=tt, schedule=s5_with)


def _mixers_vmem_bytes(s):
    kt, nst = TCH * LANES, 16 * LANES
    weights = 2 * (kt * kt + 2 * kt * nst)
    s5_rows = s * (4 * 2 * LANES) + (s // TCH) * (2 * kt + 4 * nst + 2 * nst)
    lru_rows = s * LANES * (2 * 2 * 2 + 5 * 4)
    return weights + s5_rows + lru_rows


def _mixers(proj, col_u, col_x, m, win, wout, av, conv_w, par, wg, layer, width):
    b, s, n = proj.shape
    ncb = width // LANES
    nq = s // (NSEG * TCH)
    nc = nq * NSEG
    kt = TCH * LANES
    nst = 16 * LANES
    tt = min(GATE_ROWS, s)
    assert _mixers_vmem_bytes(s) <= VMEM_BYTES, "sequence too long for one VMEM-resident channel block"
    proj5 = proj.reshape(b, nq, TCH, NSEG, n)
    ys, hb = pl.pallas_call(
        functools.partial(_mixers_body, nq=nq, s=s, tt=tt),
        grid=(ncb, b),
        in_specs=[
            pl.BlockSpec((None, nq, TCH, NSEG, LANES), lambda c, bi: (bi, 0, 0, 0, col_u + c)),
            pl.BlockSpec((None, s, LANES), lambda c, bi: (bi, 0, col_x + c)),
            _resident((None, None, kt, kt), lambda c, bi: (layer, c, 0, 0)),
            _resident((None, None, kt, nst), lambda c, bi: (layer, c, 0, 0)),
            _resident((None, None, kt, nst), lambda c, bi: (layer, c, 0, 0)),
            pl.BlockSpec((None, None, 8, 4 * LANES), lambda c, bi: (layer, c, 0, 0)),
            pl.BlockSpec((None, 4, LANES), lambda c, bi: (layer, 0, c)),
            pl.BlockSpec((None, None, 8, LANES), lambda c, bi: (layer, c, 0, 0)),
            pl.BlockSpec((None, None, LANES, 4 * LANES), lambda c, bi: (layer, c, 0, 0)),
        ],
        out_specs=[pl.BlockSpec((None, nq, TCH, NSEG, LANES), lambda c, bi: (bi, 0, 0, 0, c)),
                   pl.BlockSpec((None, s, LANES), lambda c, bi: (bi, 0, c))],
        out_shape=[jax.ShapeDtypeStruct((b, nq, TCH, NSEG, width), BF16),
                   jax.ShapeDtypeStruct((b, s, width), BF16)],
        scratch_shapes=[
            pltpu.VMEM((nc, kt), BF16),
            pltpu.VMEM((nc, nst), F32),
            pltpu.VMEM((nc, nst), BF16),
            pltpu.VMEM((NSEG, nst), F32),
            pltpu.VMEM((NSEG, nst), F32),
            pltpu.VMEM((s + 3 * NSEG, LANES), F32),
            pltpu.VMEM((3 * NSEG, LANES), F32),
            pltpu.VMEM((s, LANES), F32),
            pltpu.VMEM((s, LANES), F32),
            pltpu.VMEM((s, LANES), F32),
            pltpu.VMEM((s, LANES), F32),
        ],
        compiler_params=_cparams(("parallel", "parallel")),
        name="mixers",
    )(proj5, proj, m, win, wout, av, conv_w, par, wg)
    return ys.reshape(b, s, width), hb


def _postmix_body(x_ref, ys_ref, u_ref, hb_ref, gr_ref, ga0_ref, ga1_ref, gb0_ref, gb1_ref, d_ref,
                  wglu_ref, wa_ref, wb_ref, wo_ref, o_ref):
    y = jax.nn.gelu(ys_ref[...].astype(F32) + d_ref[...] * u_ref[...].astype(F32))
    z = jnp.dot(y.astype(BF16), wglu_ref[...], preferred_element_type=F32)
    ya = (y * _sigmoid(z)).astype(BF16)
    pa = jnp.dot(ya, wa_ref[...], preferred_element_type=F32)
    yb = (hb_ref[...].astype(F32) * jax.nn.gelu(gr_ref[...].astype(F32))).astype(BF16)
    pb = jnp.dot(yb, wb_ref[...], preferred_element_type=F32)
    half = ga0_ref.shape[-1]
    merged = [(_sigmoid(ga_ref[...].astype(F32)) * pa[:, k * half:(k + 1) * half]
               + _sigmoid(gb_ref[...].astype(F32)) * pb[:, k * half:(k + 1) * half]).astype(BF16)
              for k, (ga_ref, gb_ref) in enumerate(((ga0_ref, gb0_ref), (ga1_ref, gb1_ref)))]
    o_ref[...] = x_ref[...] + jnp.dot(jnp.concatenate(merged, axis=1), wo_ref[...],
                                      preferred_element_type=F32)


def _postmix(x, ys, proj, yb, d, wglu, wa, wb, wo, layer):
    b, s, dm = x.shape
    w5 = ys.shape[-1]
    wl = yb.shape[-1]
    tm = min(POSTMIX_ROWS, s)
    half = dm // 2
    assert w5 == half and wl == half and proj.shape[-1] == 7 * half
    row = lambda bi, i: (bi, i, 0)
    gate = lambda k: pl.BlockSpec((None, tm, half), lambda bi, i: (bi, i, k))
    return pl.pallas_call(
        _postmix_body,
        grid=(b, s // tm),
        in_specs=[
            pl.BlockSpec((None, tm, dm), row),
            pl.BlockSpec((None, tm, w5), row),
            pl.BlockSpec((None, tm, w5), row),
            pl.BlockSpec((None, tm, wl), row),
            gate(2), gate(3), gate(4), gate(5), gate(6),
            pl.BlockSpec((None, 1, w5), lambda bi, i: (layer, 0, 0)),
            _resident((None, w5, w5), lambda bi, i: (layer, 0, 0)),
            _resident((None, w5, dm), lambda bi, i: (layer, 0, 0)),
            _resident((None, wl, dm), lambda bi, i: (layer, 0, 0)),
            _resident((None, dm, dm), lambda bi, i: (layer, 0, 0)),
        ],
        out_specs=pl.BlockSpec((None, tm, dm), row),
        out_shape=jax.ShapeDtypeStruct((b, s, dm), F32),
        compiler_params=_cparams(("parallel", "parallel")),
        name="postmix",
    )(x, ys, proj, yb, proj, proj, proj, proj, proj, d, wglu, wa, wb, wo)


def _ffn_body(x_ref, xp_ref, xq_ref, g_ref, wug_ref, wuv_ref, cwg_ref, cwv_ref, wd_ref, gout_ref, o_ref,
              xn_ref, edge_ref, *, tm, norm_out):
    r = NSEG
    i = pl.program_id(1)
    f = pl.program_id(2)

    def normalise():
        g = g_ref[...]
        xn_ref[r:r + tm, :] = _rms(x_ref[...], g).astype(BF16)
        edge_ref[...] = jnp.zeros(edge_ref.shape, F32)
        prev = _rms(xp_ref[...], g)
        edge_ref[r:2 * r, :] = prev
        prev = jnp.where(i == 0, edge_ref[r - 1:2 * r - 1, :], prev)
        xn_ref[0:r, :] = prev.astype(BF16)
        nxt = _rms(xq_ref[...], g)
        edge_ref[r:2 * r, :] = nxt
        nxt = jnp.where(i == pl.num_programs(1) - 1, edge_ref[r + 1:2 * r + 1, :], nxt)
        xn_ref[r + tm:2 * r + tm, :] = nxt.astype(BF16)

    def hidden_step():
        xn = xn_ref[...]

        def conv(w_ref, cw_ref):
            h = jnp.dot(xn, w_ref[...], preferred_element_type=F32)
            return (cw_ref[0:1, :] * h[0:tm] + cw_ref[1:2, :] * h[r:r + tm]
                    + cw_ref[2:3, :] * h[2 * r:2 * r + tm])

        act = (jax.nn.gelu(conv(wug_ref, cwg_ref)) * conv(wuv_ref, cwv_ref)).astype(BF16)
        return jnp.dot(act, wd_ref[...], preferred_element_type=F32)

    @pl.when(f == 0)
    def _():
        normalise()
        o_ref[...] = x_ref[...] + hidden_step()

    @pl.when(f > 0)
    def _():
        o_ref[...] += hidden_step()

    if norm_out:
        @pl.when(pl.program_id(2) == pl.num_programs(2) - 1)
        def _():
            o_ref[...] = _rms(o_ref[...], gout_ref[...])


def _ffn(x, g, wup, cw, wdown, gout, layer, norm_out):
    b, s, dm = x.shape
    fh = wdown.shape[1]
    tm = min(FFN_ROWS, s)
    tf = min(FFN_COLS, fh)
    nf = fh // tf
    gpt = tm // NSEG
    ng = s // NSEG
    return pl.pallas_call(
        functools.partial(_ffn_body, tm=tm, norm_out=norm_out),
        grid=(b, s // tm, nf),
        in_specs=[
            _resident((None, tm, dm), lambda bi, i, f: (bi, i, 0)),
            pl.BlockSpec((None, NSEG, dm), lambda bi, i, f: (bi, lax.rem(i * gpt + ng - 1, ng), 0)),
            pl.BlockSpec((None, NSEG, dm), lambda bi, i, f: (bi, lax.rem((i + 1) * gpt, ng), 0)),
            pl.BlockSpec((None, 1, dm), lambda bi, i, f: (layer, 0, 0)),
            pl.BlockSpec((None, dm, tf), lambda bi, i, f: (layer, 0, f)),
            pl.BlockSpec((None, dm, tf), lambda bi, i, f: (layer, 0, nf + f)),
            pl.BlockSpec((None, 3, tf), lambda bi, i, f: (layer, 0, f)),
            pl.BlockSpec((None, 3, tf), lambda bi, i, f: (layer, 0, nf + f)),
            pl.BlockSpec((None, tf, dm), lambda bi, i, f: (layer, f, 0)),
            pl.BlockSpec((1, dm), lambda bi, i, f: (0, 0)),
        ],
        out_specs=pl.BlockSpec((None, tm, dm), lambda bi, i, f: (bi, i, 0)),
        out_shape=jax.ShapeDtypeStruct((b, s, dm), F32),
        scratch_shapes=[pltpu.VMEM((tm + 2 * NSEG, dm), BF16), pltpu.VMEM((3 * NSEG, dm), F32)],
        compiler_params=_cparams(("parallel", "parallel", "arbitrary")),
        name="ffn",
    )(x, x, x, g, wup, wup, cw, cw, wdown, gout)


def _block_diag_embed(w, nblk, spec_in, spec_out):
    return jnp.einsum(spec_in + ",gh->" + spec_out, w, jnp.eye(nblk, dtype=w.dtype))


def _s5_prep_body(tab_ref, bb_ref, cc_ref, m_ref, win_ref, woutt_ref, p_ref, q_ref, d_ref, *, npz):
    t = TCH
    ns = tab_ref.shape[-1]
    lane = lax.broadcasted_iota(jnp.int32, (S5_H, 2 * ns), 1)
    grp = (lane % ns) // npz
    nt = (((1,), (1,)), ((), ()))

    def spread(dst, compact):
        for g in range(GROUPS_PER_BLOCK):
            dst[g * S5_H:(g + 1) * S5_H, :] = jnp.where(grp == g, compact, 0.0)

    for d in range(2):
        br, bi = bb_ref[d, 0], bb_ref[d, 1]
        cr, ci = cc_ref[d, 0], cc_ref[d, 1]
        for k in range(t + 1):
            ar, ai = tab_ref[d, 0, k:k + 1, :], tab_ref[d, 1, k:k + 1, :]
            if k < t:
                spread(p_ref.at[d, k], jnp.concatenate([ar * br - ai * bi, ar * bi + ai * br], axis=1))
            spread(q_ref.at[d, k], jnp.concatenate([cr * ar - ci * ai, -(cr * ai + ci * ar)], axis=1))
        pk, q0 = p_ref[d].reshape(t * LANES, 2 * ns), q_ref[d, 0]
        ph, qh = pk.astype(BF16), q0.astype(BF16)
        pl_, ql = (pk - ph.astype(F32)).astype(BF16), (q0 - qh.astype(F32)).astype(BF16)
        taps = (lax.dot_general(ph, qh, nt, preferred_element_type=F32)
                + lax.dot_general(ph, ql, nt, preferred_element_type=F32)
                + lax.dot_general(pl_, qh, nt, preferred_element_type=F32))
        d_ref[d] = taps.reshape(t, LANES, LANES)

    for s in range(t):
        rows = slice(s * LANES, (s + 1) * LANES)
        for to in range(t):
            if to > s:
                blk = d_ref[0, to - s]
            elif to < s:
                blk = d_ref[1, s - to]
            else:
                blk = d_ref[0, 0] + d_ref[1, 0]
            m_ref[rows, to * LANES:(to + 1) * LANES] = blk.astype(BF16)
        win_ref[rows, 0:2 * ns] = p_ref[0, t - 1 - s].astype(BF16)
        win_ref[rows, 2 * ns:4 * ns] = p_ref[1, s].astype(BF16)
        woutt_ref[rows, 0:2 * ns] = q_ref[0, s + 1].astype(BF16)
        woutt_ref[rows, 2 * ns:4 * ns] = q_ref[1, t - s].astype(BF16)


def _prep_s5(a_re, a_im, log_dt, b_re, b_im, c_re, c_im, seg_lens):
    nl, _, ng, npz = a_re.shape
    gb = GROUPS_PER_BLOCK
    ncb = ng // gb
    ns = gb * npz
    kt = TCH * LANES
    lr, li = a_re.astype(F32), a_im.astype(F32)
    dt = jnp.exp(log_dt.astype(F32))[..., None]
    zr, zi = lr * dt, li * dt

    def cpow(k, re=zr, im=zi):
        mag = jnp.exp(re * k)
        return mag * jnp.cos(im * k), mag * jnp.sin(im * k)

    kv = jnp.arange(S5_H, dtype=F32)[:, None]
    tab = jnp.stack(cpow(kv, zr[..., None, :], zi[..., None, :]), axis=2)
    ar, ai = cpow(1.0)
    den = lr * lr + li * li
    fr = ((ar - 1.0) * lr + ai * li) / den
    fi = (ai * lr - (ar - 1.0) * li) / den
    b_re, b_im = b_re.astype(F32), b_im.astype(F32)
    bbr = fr[..., None] * b_re - fi[..., None] * b_im
    bbi = fr[..., None] * b_im + fi[..., None] * b_re
    bb = jnp.swapaxes(jnp.stack([bbr, bbi], axis=2), -1, -2)
    cc = jnp.stack([c_re.astype(F32), c_im.astype(F32)], axis=2)

    def lanes(v):
        v = v.reshape(nl, 2, 2, ncb, gb, S5_H, npz).transpose(0, 3, 1, 2, 5, 4, 6)
        return v.reshape(nl, ncb, 2, 2, S5_H, ns)

    tab_spec = pl.BlockSpec((None, None, 2, 2, S5_H, ns), lambda l, c: (l, c, 0, 0, 0, 0))
    w_spec = pl.BlockSpec((None, None, kt, 4 * ns), lambda l, c: (l, c, 0, 0))
    m, win, woutt = pl.pallas_call(
        functools.partial(_s5_prep_body, npz=npz),
        grid=(nl, ncb),
        in_specs=[tab_spec, tab_spec, tab_spec],
        out_specs=[pl.BlockSpec((None, None, kt, kt), lambda l, c: (l, c, 0, 0)), w_spec, w_spec],
        out_shape=[jax.ShapeDtypeStruct((nl, ncb, kt, kt), BF16),
                   jax.ShapeDtypeStruct((nl, ncb, kt, 4 * ns), BF16),
                   jax.ShapeDtypeStruct((nl, ncb, kt, 4 * ns), BF16)],
        scratch_shapes=[pltpu.VMEM((2, TCH, LANES, 2 * ns), F32),
                        pltpu.VMEM((2, TCH + 1, LANES, 2 * ns), F32),
                        pltpu.VMEM((2, TCH, LANES, LANES), F32)],
        compiler_params=_cparams(("parallel", "parallel")),
        name="s5_prep",
    )(lanes(tab), lanes(bb), lanes(cc))

    def vec(vr, vi):
        v = jnp.stack([vr[:, 0], vi[:, 0], vr[:, 1], vi[:, 1]], axis=1)
        return v.reshape(nl, 4, ncb, ns).transpose(0, 2, 1, 3)

    avs = [jnp.concatenate([vec(*cpow(float(TCH))), vec(*cpow(float(ls)))], axis=2) for ls in seg_lens]
    return m, win, woutt, avs


def _prep_lru(conv_b, w_a, b_a, w_x, b_x, lam):
    nl, _, nb, blk, _ = w_a.shape
    width = nb * blk
    ncb = width // LANES
    per = LANES // blk

    def bd(w):
        w = w.reshape(nl, ncb, per, blk, blk)
        return _block_diag_embed(w, per, "lcgij", "lcgihj").reshape(nl, ncb, LANES, LANES)

    wg = jnp.concatenate([bd(w_a[:, 0]), bd(w_x[:, 0]), bd(w_a[:, 1]), bd(w_x[:, 1])], axis=-1)
    rows = [conv_b, b_a[:, 0], b_x[:, 0], b_a[:, 1], b_x[:, 1], lam[:, 0], lam[:, 1],
            jnp.zeros_like(conv_b)]
    par = jnp.stack(rows, axis=1).astype(F32)
    par = par.reshape(nl, 8, ncb, LANES).transpose(0, 2, 1, 3)
    return (0.5 * wg).astype(BF16), par


def kernel(x_prompt, x_sample, norm1_g, w_in, s5_a_re, s5_a_im, s5_log_dt, s5_b_re, s5_b_im, s5_c_re, s5_c_im, s5_d, s5_w_glu, lru_conv_w, lru_conv_b, lru_w_a, lru_b_a, lru_w_x, lru_b_x, lru_lambda, w_proj_a, w_proj_b, w_out, norm2_g, ffn_w_up, ffn_conv_w, ffn_w_down, final_g):
    depth, dm, _ = w_in.shape
    w5 = s5_d.shape[-1]
    wl = lru_conv_b.shape[-1]
    groups = (x_prompt, x_sample)
    seg_lens = [x.shape[1] // NSEG for x in groups]

    w_in_p = w_in.astype(BF16)
    col_u = 0
    col_x = w5 // LANES
    m, win, wout, avs = _prep_s5(s5_a_re, s5_a_im, s5_log_dt, s5_b_re, s5_b_im, s5_c_re, s5_c_im, seg_lens)
    wg, par = _prep_lru(lru_conv_b, lru_w_a, lru_b_a, lru_w_x, lru_b_x, lru_lambda)
    conv_w = lru_conv_w.astype(F32)
    wglu, wa, wb, wo = (w.astype(BF16) for w in (s5_w_glu, w_proj_a, w_proj_b, w_out))
    wup, wdown = ffn_w_up.astype(BF16), ffn_w_down.astype(BF16)
    ffn_cw = ffn_conv_w.astype(F32)
    g1 = norm1_g.astype(F32)[:, None, :]
    g2 = norm2_g.astype(F32)[:, None, :]
    d5 = s5_d.astype(F32)[:, None, :]
    gfin = final_g.astype(F32)[None, :]

    outs = []
    for x, av in zip(groups, avs):
        x = _interleave(x.astype(F32))
        for l in range(depth):
            proj = _inproj(x, g1, w_in_p, l)
            ys, hb = _mixers(proj, col_u, col_x, m, win, wout, av, conv_w, par, wg, l, w5)
            x = _postmix(x, ys, proj, hb, d5, wglu, wa, wb, wo, l)
            x = _ffn(x, g2, wup, ffn_cw, wdown, gfin, l, norm_out=(l == depth - 1))
        outs.append(_deinterleave(x))
    return tuple(outs)
```
